```python
import jax, jax.numpy as jnp
from jax import lax
import numpy as np

D_MODEL = 1024
BATCH = 4
SEQ = 4096
DEPTH = 1
DEC_BATCH = 16
DEC_SEQ = 32
PAST_LEN = 1024

CHUNK = 64
N_HEADS = 16
N_KV_HEADS = 2
HEAD_DIM = 64
Q_PER_KV = N_HEADS // N_KV_HEADS
WINDOW = 128
WINDOW_CHUNKS = WINDOW // CHUNK
D_POOL = D_MODEL // 2
N_POOL_GROUPS = 4
POOL_GROUP = D_POOL // N_POOL_GROUPS
POOL_WINDOWS = (2, 4, 8, 16)
POOL_HIST = max(POOL_WINDOWS) - 1
D_Q = N_HEADS * HEAD_DIM
D_KV = N_KV_HEADS * HEAD_DIM
D_IN = D_POOL + D_Q + 2 * D_KV + 2 * D_MODEL
SPLITS = (D_POOL, D_POOL + D_Q, D_POOL + D_Q + D_KV, D_POOL + D_Q + 2 * D_KV, D_POOL + D_Q + 2 * D_KV + D_MODEL)
D_FF = 2816
EPS = 1e-6

kernel_name = "streaming_pool_swa_hybrid_step"


def rmsnorm(x, g):
    xf = x.astype(jnp.float32)
    y = xf * lax.rsqrt(jnp.mean(xf * xf, axis=-1, keepdims=True) + EPS)
    return (y * g.astype(jnp.float32)).astype(x.dtype)


def half_swiglu(x, g, w_in, w_out):
    h = rmsnorm(x, g)
    gate, up = jnp.split(h @ w_in, 2, axis=-1)
    return x + 0.5 * ((jax.nn.silu(gate) * up) @ w_out)


def multiscale_pool(u, hist, w_group, b_group, scale):
    B, S, _ = u.shape
    uf = u.astype(jnp.float32)
    if hist is None:
        ext = jnp.pad(uf, ((0, 0), (POOL_HIST, 0), (0, 0)))
    else:
        ext = jnp.concatenate([hist.astype(jnp.float32), uf], axis=1)
    cs = jnp.pad(jnp.cumsum(ext, axis=1), ((0, 0), (1, 0), (0, 0)))
    end = cs[:, POOL_HIST + 1:]
    outs = []
    for gi, w in enumerate(POOL_WINDOWS):
        sl = slice(gi * POOL_GROUP, (gi + 1) * POOL_GROUP)
        s = end[..., sl] - cs[:, POOL_HIST + 1 - w:POOL_HIST + 1 - w + S, sl]
        if hist is None:
            cnt = jnp.minimum(jnp.arange(1, S + 1), w).astype(jnp.float32)[None, :, None]
        else:
            cnt = jnp.float32(w)
        outs.append(s / cnt - uf[..., sl])
    pooled = jnp.concatenate(outs, axis=-1).reshape(B, S, N_POOL_GROUPS, POOL_GROUP)
    y = jnp.einsum('bsgc,gcd->bsgd', pooled, w_group.astype(jnp.float32)) + b_group.astype(jnp.float32)
    y = y.reshape(B, S, D_POOL) * scale.astype(jnp.float32)
    return y.astype(u.dtype)


def attend_with_sinks(qg, k, v, sinks, mask):
    s = jnp.einsum('...qkgd,...pkd->...kgqp', qg.astype(jnp.float32), k.astype(jnp.float32)) * (HEAD_DIM ** -0.5)
    if mask is not None:
        s = jnp.where(mask, s, jnp.finfo(jnp.float32).min)
    sink = sinks.astype(jnp.float32).reshape(N_KV_HEADS, Q_PER_KV)[:, :, None, None]
    m = jnp.maximum(jnp.max(s, axis=-1, keepdims=True), sink)
    p = jnp.exp(s - m)
    w = p / (jnp.sum(p, axis=-1, keepdims=True) + jnp.exp(sink - m))
    o = jnp.einsum('...kgqp,...pkd->...qkgd', w, v.astype(jnp.float32))
    return o.astype(qg.dtype)


def swa_prompt(q, k, v, sinks):
    B, S = q.shape[:2]
    nb = S // CHUNK
    qb = q.reshape(B, nb, CHUNK, N_KV_HEADS, Q_PER_KV, HEAD_DIM)
    pad = ((0, 0), (WINDOW, 0), (0, 0), (0, 0))
    kp = jnp.pad(k, pad).reshape(B, nb + WINDOW_CHUNKS, CHUNK, N_KV_HEADS, HEAD_DIM)
    vp = jnp.pad(v, pad).reshape(B, nb + WINDOW_CHUNKS, CHUNK, N_KV_HEADS, HEAD_DIM)
    kb = jnp.concatenate([kp[:, j:j + nb] for j in range(WINDOW_CHUNKS + 1)], axis=2)
    vb = jnp.concatenate([vp[:, j:j + nb] for j in range(WINDOW_CHUNKS + 1)], axis=2)
    key_chunk = jnp.arange(nb)[:, None] + jnp.arange((WINDOW_CHUNKS + 1) * CHUNK)[None, :] // CHUNK - WINDOW_CHUNKS
    mask = (key_chunk >= 0)[None, :, None, None, None, :]
    o = attend_with_sinks(qb, kb, vb, sinks, mask)
    return o.reshape(B, S, D_Q)


def swa_sample(q, k, v, k_hist, v_hist, sinks):
    B, T = q.shape[:2]
    kf = jnp.concatenate([k_hist.astype(k.dtype), k], axis=1)
    vf = jnp.concatenate([v_hist.astype(v.dtype), v], axis=1)
    o = attend_with_sinks(q.reshape(B, T, N_KV_HEADS, Q_PER_KV, HEAD_DIM), kf, vf, sinks, None)
    return o.reshape(B, T, D_Q), kf[:, -WINDOW:], vf[:, -WINDOW:]


def layer(x, hist, p):
    (norm_ffn1, ffn1_w_in, ffn1_w_out, norm_mix, w_in, b_gate, pool_w, pool_b, pool_scale,
     q_norm, k_norm, sinks, w_pool_proj, w_attn_proj, w_out, norm_ffn2, ffn2_w_in, ffn2_w_out) = p
    B, S, _ = x.shape
    x = half_swiglu(x, norm_ffn1, ffn1_w_in, ffn1_w_out)
    h = rmsnorm(x, norm_mix)
    u, q, k, v, ga, gb = jnp.split(h @ w_in, SPLITS, axis=-1)
    q = rmsnorm(q.reshape(B, S, N_HEADS, HEAD_DIM), q_norm)
    k = rmsnorm(k.reshape(B, S, N_KV_HEADS, HEAD_DIM), k_norm)
    v = v.reshape(B, S, N_KV_HEADS, HEAD_DIM)
    if hist is None:
        ya = multiscale_pool(u, None, pool_w, pool_b, pool_scale)
        yb = swa_prompt(q, k, v, sinks)
        new_pool, new_k, new_v = u[:, -POOL_HIST:], k[:, -WINDOW:], v[:, -WINDOW:]
    else:
        pool_hist, k_hist, v_hist = hist
        ya = multiscale_pool(u, pool_hist, pool_w, pool_b, pool_scale)
        yb, new_k, new_v = swa_sample(q, k, v, k_hist, v_hist, sinks)
        new_pool = jnp.concatenate([pool_hist.astype(u.dtype), u], axis=1)[:, -POOL_HIST:]
    g_a = jax.nn.sigmoid(ga + b_gate[0])
    g_b = jax.nn.sigmoid(gb + b_gate[1])
    x = x + (g_a * (ya @ w_pool_proj) + g_b * (yb @ w_attn_proj)) @ w_out
    x = half_swiglu(x, norm_ffn2, ffn2_w_in, ffn2_w_out)
    return x, new_pool, new_k, new_v


def setup_inputs(seed: int = 0) -> dict:
    key = jax.random.key(seed)
    ks = jax.random.split(key, 32)
    nrm = lambda i, shape, s: jax.random.normal(ks[i], shape, jnp.float32) * s
    gain = lambda i, shape: 1.0 + 0.05 * jax.random.normal(ks[i], shape, jnp.float32)
    L = DEPTH
    return {
        "x_prompt": nrm(0, (BATCH, SEQ, D_MODEL), 1.0),
        "x_sample": nrm(1, (DEC_BATCH, DEC_SEQ, D_MODEL), 1.0),
        "state_pool": nrm(2, (L, DEC_BATCH, POOL_HIST, D_POOL), 1.0),
        "cache_k": nrm(3, (L, DEC_BATCH, WINDOW, N_KV_HEADS, HEAD_DIM), 1.0),
        "cache_v": nrm(4, (L, DEC_BATCH, WINDOW, N_KV_HEADS, HEAD_DIM), 1.0),
        "norm_ffn1": gain(5, (L, D_MODEL)),
        "ffn1_w_in": nrm(6, (L, D_MODEL, 2 * D_FF), D_MODEL ** -0.5),
        "ffn1_w_out": nrm(7, (L, D_FF, D_MODEL), D_FF ** -0.5),
        "norm_mix": gain(8, (L, D_MODEL)),
        "w_in": nrm(9, (L, D_MODEL, D_IN), D_MODEL ** -0.5),
        "b_gate": nrm(10, (L, 2, D_MODEL), 0.1),
        "pool_w": nrm(11, (L, N_POOL_GROUPS, POOL_GROUP, POOL_GROUP), POOL_GROUP ** -0.5),
        "pool_b": nrm(12, (L, N_POOL_GROUPS, POOL_GROUP), 0.02),
        "pool_scale": gain(13, (L, D_POOL)),
        "q_norm": gain(14, (L, HEAD_DIM)),
        "k_norm": gain(15, (L, HEAD_DIM)),
        "sinks": nrm(16, (L, N_HEADS), 0.5),
        "w_pool_proj": nrm(17, (L, D_POOL, D_MODEL), D_POOL ** -0.5),
        "w_attn_proj": nrm(18, (L, D_Q, D_MODEL), D_Q ** -0.5),
        "w_out": nrm(19, (L, D_MODEL, D_MODEL), D_MODEL ** -0.5),
        "norm_ffn2": gain(20, (L, D_MODEL)),
        "ffn2_w_in": nrm(21, (L, D_MODEL, 2 * D_FF), D_MODEL ** -0.5),
        "ffn2_w_out": nrm(22, (L, D_FF, D_MODEL), D_FF ** -0.5),
    }


def reference(x_prompt, x_sample, state_pool, cache_k, cache_v, norm_ffn1, ffn1_w_in, ffn1_w_out,
              norm_mix, w_in, b_gate, pool_w, pool_b, pool_scale, q_norm, k_norm, sinks,
              w_pool_proj, w_attn_proj, w_out, norm_ffn2, ffn2_w_in, ffn2_w_out):
    xp, xs = x_prompt, x_sample
    pp, kp, vp, ps, ksm, vsm = [], [], [], [], [], []
    for l in range(DEPTH):
        p = (norm_ffn1[l], ffn1_w_in[l], ffn1_w_out[l], norm_mix[l], w_in[l], b_gate[l], pool_w[l],
             pool_b[l], pool_scale[l], q_norm[l], k_norm[l], sinks[l], w_pool_proj[l], w_attn_proj[l],
             w_out[l], norm_ffn2[l], ffn2_w_in[l], ffn2_w_out[l])
        xp, a, b, c = layer(xp, None, p)
        pp.append(a); kp.append(b); vp.append(c)
        xs, a, b, c = layer(xs, (state_pool[l], cache_k[l], cache_v[l]), p)
        ps.append(a); ksm.append(b); vsm.append(c)
    return (xp, xs, jnp.stack(pp), jnp.stack(kp), jnp.stack(vp), jnp.stack(ps), jnp.stack(ksm), jnp.stack(vsm))
```

```python
import functools

import jax
import jax.numpy as jnp
from jax import lax
from jax.experimental import pallas as pl
from jax.experimental.pallas import tpu as pltpu

D_MODEL = 1024
CHUNK = 64
N_HEADS = 16
N_KV_HEADS = 2
HEAD_DIM = 64
WINDOW = 128
D_POOL = 512
N_POOL_GROUPS = 4
POOL_GROUP = 128
POOL_WINDOWS = (2, 4, 8, 16)
POOL_HIST = 15
POOL_HIST_PAD = 16
D_Q = N_HEADS * HEAD_DIM
D_KV = N_KV_HEADS * HEAD_DIM
D_UQKV = D_POOL + D_Q + 2 * D_KV
D_FF = 2816
EPS = 1e-6

LANES = 128
FF_CHUNK = 256
N_FF_CHUNKS = D_FF // FF_CHUNK
PAIRS_PER_KV = (N_HEADS // N_KV_HEADS) // 2
VMEM_LIMIT_BYTES = 56 * 1024 * 1024
NEG_BIG = float(jnp.finfo(jnp.float32).min)

BF16 = jnp.bfloat16
F32 = jnp.float32


def _dot(a, b):
    return jnp.dot(a, b, preferred_element_type=F32)


def _dot_nt(a, b):
    return lax.dot_general(a, b, (((1,), (1,)), ((), ())), preferred_element_type=F32)


def _rmsnorm(x, g):
    return x * lax.rsqrt(jnp.mean(x * x, axis=-1, keepdims=True) + EPS) * g


def _ffn(x, g_ref, w1_ref, w2_ref, a_scr):
    h = _rmsnorm(x, g_ref[...]).astype(BF16)
    for c in range(N_FF_CHUNKS):
        gu = _dot(h, w1_ref[:, c * 2 * FF_CHUNK:(c + 1) * 2 * FF_CHUNK])
        gate = gu[:, :FF_CHUNK]
        up = gu[:, FF_CHUNK:]
        a_scr[:, c * FF_CHUNK:(c + 1) * FF_CHUNK] = (gate * jax.nn.sigmoid(gate) * up).astype(BF16)
    return x + 0.5 * _dot(a_scr[...], w2_ref[...])


def _half_sumsq(x):
    y = x * x
    lane = lax.broadcasted_iota(jnp.int32, y.shape, 1)
    for s in (1, 2, 4, 8, 16, 32):
        from_lower = pltpu.roll(y, s, axis=1)
        from_upper = pltpu.roll(y, LANES - s, axis=1)
        y = y + jnp.where((lane & s) != 0, from_lower, from_upper)
    return y


def _head_norm(x, g2):
    return x * lax.rsqrt(_half_sumsq(x) * (1.0 / HEAD_DIM) + EPS) * g2


def _front_body(x_ref, g1_ref, w1_ref, w2_ref, gm_ref, win_ref, qn_ref, kn_ref,
                x1_ref, h_ref, u_ref, q_ref, kv_ref, a_scr):
    x1 = _ffn(x_ref[...], g1_ref, w1_ref, w2_ref, a_scr)
    x1_ref[...] = x1
    h = _rmsnorm(x1, gm_ref[...]).astype(BF16)
    h_ref[...] = h
    z = _dot(h, win_ref[...])
    u_ref[...] = z[:, :D_POOL]
    qg = qn_ref[...] * (HEAD_DIM ** -0.5)
    for p in range(D_Q // LANES):
        lo = D_POOL + p * LANES
        q_ref[:, p * LANES:(p + 1) * LANES] = _head_norm(z[:, lo:lo + LANES], qg).astype(BF16)
    k0 = D_POOL + D_Q
    kv_ref[:, :D_KV] = _head_norm(z[:, k0:k0 + D_KV], kn_ref[...])
    kv_ref[:, D_KV:] = z[:, k0 + D_KV:k0 + 2 * D_KV]


def _ffn_body(x_ref, g_ref, w1_ref, w2_ref, o_ref, a_scr):
    o_ref[...] = _ffn(x_ref[...], g_ref, w1_ref, w2_ref, a_scr)


def _pool_mixer(uext_ref, rows, first_row, pw_ref, pb_ref, ps_ref):
    u_new = uext_ref[POOL_HIST_PAD:, :]
    outs = []
    for gi, w in enumerate(POOL_WINDOWS):
        sl = slice(gi * POOL_GROUP, (gi + 1) * POOL_GROUP)
        s = u_new[:, sl]
        for i in range(1, w):
            s = s + uext_ref[POOL_HIST_PAD - i:POOL_HIST_PAD - i + rows, sl]
        if first_row is None:
            mean = s * (1.0 / w)
        else:
            pos = first_row + lax.broadcasted_iota(jnp.int32, (rows, 1), 0)
            mean = s / jnp.minimum(pos + 1, w).astype(F32)
        pooled = (mean - u_new[:, sl]).astype(BF16)
        y = _dot(pooled, pw_ref[gi]) + pb_ref[:, sl]
        outs.append(y * ps_ref[:, sl])
    return jnp.concatenate(outs, axis=-1)


def _fill_kv_variants(kvx_ref, row0, kv):
    rows = kv.shape[0]
    lane = lax.broadcasted_iota(jnp.int32, (rows, LANES), 1)
    low = lane < HEAD_DIM
    for t, base in ((0, 0), (1, D_KV)):
        both = kv[:, base:base + D_KV]
        swapped = pltpu.roll(both, HEAD_DIM, axis=1)
        variants = (jnp.where(low, both, 0.0), jnp.where(low, 0.0, swapped),
                    jnp.where(low, swapped, 0.0), jnp.where(low, 0.0, both))
        for i, val in enumerate(variants):
            kvx_ref[4 * t + i, row0:row0 + rows, :] = val.astype(BF16)


def _attend_block(q_ref, q_row0, q_rows, kvx_ref, key_row0, n_keys, key_valid, sinks_ref, yb_ref):
    m_rows = PAIRS_PER_KV * q_rows
    row_blk = lax.broadcasted_iota(jnp.int32, (m_rows, 1), 0) // q_rows
    lane = lax.broadcasted_iota(jnp.int32, (m_rows, LANES), 1)
    for kh in range(N_KV_HEADS):
        qs = jnp.concatenate(
            [q_ref[q_row0:q_row0 + q_rows, (kh * PAIRS_PER_KV + pp) * LANES:(kh * PAIRS_PER_KV + pp + 1) * LANES]
             for pp in range(PAIRS_PER_KV)], axis=0)
        o = None
        inv = []
        for ab in range(2):
            sink = jnp.zeros((m_rows, 1), F32)
            for pp in range(PAIRS_PER_KV):
                sink = jnp.where(row_blk == pp, sinks_ref[kh * 2 * PAIRS_PER_KV + 2 * pp + ab], sink)
            k_op = kvx_ref[2 * kh + ab, key_row0:key_row0 + n_keys, :]
            v_op = kvx_ref[4 + 2 * kh + ab, key_row0:key_row0 + n_keys, :]
            s = _dot_nt(qs, k_op)
            if key_valid is not None:
                s = jnp.where(key_valid, s, NEG_BIG)
            m = jnp.maximum(jnp.max(s, axis=-1, keepdims=True), sink)
            p = jnp.exp(s - m)
            denom = jnp.sum(p, axis=-1, keepdims=True) + jnp.exp(sink - m)
            inv.append(1.0 / denom)
            pv = _dot(p.astype(BF16), v_op)
            o = pv if o is None else o + pv
        o = o * jnp.where(lane < HEAD_DIM, inv[0], inv[1])
        for pp in range(PAIRS_PER_KV):
            col = (kh * PAIRS_PER_KV + pp) * LANES
            yb_ref[q_row0:q_row0 + q_rows, col:col + LANES] = o[pp * q_rows:(pp + 1) * q_rows].astype(BF16)


def _mix_out(x1, h, ya, yb, wg_ref, bg_ref, wpp_ref, wap_ref, wo_ref):
    gates = jax.nn.sigmoid(_dot(h, wg_ref[...]) + bg_ref[...])
    a = _dot(ya.astype(BF16), wpp_ref[...])
    b = _dot(yb, wap_ref[...])
    mixed = gates[:, :D_MODEL] * a + gates[:, D_MODEL:] * b
    return x1 + _dot(mixed.astype(BF16), wo_ref[...])


def _prompt_mix_body(sinks_ref, x1_ref, h_ref, u_ref, q_ref, kv_ref,
                     pw_ref, pb_ref, ps_ref, wg_ref, bg_ref, wpp_ref, wap_ref, wo_ref,
                     g2_ref, w1_ref, w2_ref,
                     o_ref, uext, kvx, yb_scr, a_scr, *, rows):
    t = pl.program_id(1)

    @pl.when(t == 0)
    def _():
        uext[:POOL_HIST_PAD, :] = jnp.zeros((POOL_HIST_PAD, D_POOL), F32)
        kvx[:, :WINDOW, :] = jnp.zeros((8, WINDOW, LANES), BF16)

    uext[POOL_HIST_PAD:, :] = u_ref[...]
    _fill_kv_variants(kvx, WINDOW, kv_ref[...])
    ya = _pool_mixer(uext, rows, t * rows, pw_ref, pb_ref, ps_ref)

    n_keys = WINDOW + CHUNK
    key_chunk = lax.broadcasted_iota(jnp.int32, (1, n_keys), 1) // CHUNK
    for j in range(rows // CHUNK):
        first_chunk = t * (rows // CHUNK) + j - WINDOW // CHUNK
        _attend_block(q_ref, j * CHUNK, CHUNK, kvx, j * CHUNK, n_keys, key_chunk + first_chunk >= 0,
                      sinks_ref, yb_scr)

    x2 = _mix_out(x1_ref[...], h_ref[...], ya, yb_scr[...], wg_ref, bg_ref, wpp_ref, wap_ref, wo_ref)
    o_ref[...] = _ffn(x2, g2_ref, w1_ref, w2_ref, a_scr)

    uext[:POOL_HIST_PAD, :] = uext[rows:rows + POOL_HIST_PAD, :]
    kvx[:, :WINDOW, :] = kvx[:, rows:rows + WINDOW, :]


def _sample_mix_body(sinks_ref, x1_ref, h_ref, u_ref, q_ref, kv_ref, hu_ref, hkv_ref,
                     pw_ref, pb_ref, ps_ref, wg_ref, bg_ref, wpp_ref, wap_ref, wo_ref,
                     o_ref, uext, kvx, yb_scr, *, rows):
    uext[:POOL_HIST_PAD, :] = hu_ref[...]
    uext[POOL_HIST_PAD:, :] = u_ref[...]
    _fill_kv_variants(kvx, 0, hkv_ref[...])
    _fill_kv_variants(kvx, WINDOW, kv_ref[...])
    ya = _pool_mixer(uext, rows, None, pw_ref, pb_ref, ps_ref)
    _attend_block(q_ref, 0, rows, kvx, 0, WINDOW + rows, None, sinks_ref, yb_scr)
    o_ref[...] = _mix_out(x1_ref[...], h_ref[...], ya, yb_scr[...], wg_ref, bg_ref, wpp_ref, wap_ref, wo_ref)


def _const_spec(shape):
    return pl.BlockSpec(shape, lambda *_: (0,) * len(shape), pipeline_mode=pl.Buffered(1))


def _row_tile(n_rows):
    return 256 if n_rows % 256 == 0 else n_rows


def _front_call(x2d, g1, w1, w2, gm, win, qn2, kn2):
    n = x2d.shape[0]
    tm = _row_tile(n)
    row = lambda width: pl.BlockSpec((tm, width), lambda i: (i, 0))
    return pl.pallas_call(
        _front_body,
        grid=(n // tm,),
        in_specs=[row(D_MODEL), _const_spec(g1.shape), _const_spec(w1.shape), _const_spec(w2.shape),
                  _const_spec(gm.shape), _const_spec(win.shape), _const_spec(qn2.shape), _const_spec(kn2.shape)],
        out_specs=[row(D_MODEL), row(D_MODEL), row(D_POOL), row(D_Q), row(2 * D_KV)],
        out_shape=[jax.ShapeDtypeStruct((n, D_MODEL), F32), jax.ShapeDtypeStruct((n, D_MODEL), BF16),
                   jax.ShapeDtypeStruct((n, D_POOL), F32), jax.ShapeDtypeStruct((n, D_Q), BF16),
                   jax.ShapeDtypeStruct((n, 2 * D_KV), F32)],
        scratch_shapes=[pltpu.VMEM((tm, D_FF), BF16)],
        compiler_params=pltpu.CompilerParams(dimension_semantics=("parallel",),
                                             vmem_limit_bytes=VMEM_LIMIT_BYTES),
        name="front",
    )(x2d, g1, w1, w2, gm, win, qn2, kn2)


def _ffn_call(x2d, g, w1, w2):
    n = x2d.shape[0]
    tm = _row_tile(n)
    row = pl.BlockSpec((tm, D_MODEL), lambda i: (i, 0))
    return pl.pallas_call(
        _ffn_body,
        grid=(n // tm,),
        in_specs=[row, _const_spec(g.shape), _const_spec(w1.shape), _const_spec(w2.shape)],
        out_specs=row,
        out_shape=jax.ShapeDtypeStruct((n, D_MODEL), F32),
        scratch_shapes=[pltpu.VMEM((tm, D_FF), BF16)],
        compiler_params=pltpu.CompilerParams(dimension_semantics=("parallel",),
                                             vmem_limit_bytes=VMEM_LIMIT_BYTES),
        name="ffn",
    )(x2d, g, w1, w2)


def _prompt_mix_call(sinks, x1, h, u, q, kv, mix_w, ffn_w):
    b, s, _ = x1.shape
    tm = _row_tile(s)
    row = lambda width: pl.BlockSpec((None, tm, width), lambda i, t: (i, t, 0))
    weights = tuple(mix_w) + tuple(ffn_w)
    return pl.pallas_call(
        functools.partial(_prompt_mix_body, rows=tm),
        grid=(b, s // tm),
        in_specs=[pl.BlockSpec(memory_space=pltpu.SMEM),
                  row(D_MODEL), row(D_MODEL), row(D_POOL), row(D_Q), row(2 * D_KV)]
                 + [_const_spec(w.shape) for w in weights],
        out_specs=row(D_MODEL),
        out_shape=jax.ShapeDtypeStruct((b, s, D_MODEL), F32),
        scratch_shapes=[pltpu.VMEM((POOL_HIST_PAD + tm, D_POOL), F32),
                        pltpu.VMEM((8, WINDOW + tm, LANES), BF16),
                        pltpu.VMEM((tm, D_Q), BF16),
                        pltpu.VMEM((tm, D_FF), BF16)],
        compiler_params=pltpu.CompilerParams(dimension_semantics=("arbitrary", "arbitrary"),
                                             vmem_limit_bytes=VMEM_LIMIT_BYTES),
        name="prompt_mix",
    )(sinks, x1, h, u, q, kv, *weights)


def _sample_mix_call(sinks, x1, h, u, q, kv, hist_u, hist_kv, mix_w):
    b, t, _ = x1.shape
    row = lambda rows, width: pl.BlockSpec((None, rows, width), lambda i: (i, 0, 0))
    return pl.pallas_call(
        functools.partial(_sample_mix_body, rows=t),
        grid=(b,),
        in_specs=[pl.BlockSpec(memory_space=pltpu.SMEM),
                  row(t, D_MODEL), row(t, D_MODEL), row(t, D_POOL), row(t, D_Q), row(t, 2 * D_KV),
                  row(POOL_HIST_PAD, D_POOL), row(WINDOW, 2 * D_KV)]
                 + [_const_spec(w.shape) for w in mix_w],
        out_specs=row(t, D_MODEL),
        out_shape=jax.ShapeDtypeStruct((b, t, D_MODEL), F32),
        scratch_shapes=[pltpu.VMEM((POOL_HIST_PAD + t, D_POOL), F32),
                        pltpu.VMEM((8, WINDOW + t, LANES), BF16),
                        pltpu.VMEM((t, D_Q), BF16)],
        compiler_params=pltpu.CompilerParams(dimension_semantics=("arbitrary",),
                                             vmem_limit_bytes=VMEM_LIMIT_BYTES),
        name="sample_mix",
    )(sinks, x1, h, u, q, kv, hist_u, hist_kv, *mix_w)


def _ffn_weights(norm, w_in, w_out):
    w1 = w_in.reshape(D_MODEL, 2, N_FF_CHUNKS, FF_CHUNK).transpose(0, 2, 1, 3).reshape(D_MODEL, 2 * D_FF)
    return norm.reshape(1, D_MODEL), w1.astype(BF16), w_out.astype(BF16)


def kernel(x_prompt, x_sample, state_pool, cache_k, cache_v, norm_ffn1, ffn1_w_in, ffn1_w_out, norm_mix, w_in, b_gate, pool_w, pool_b, pool_scale, q_norm, k_norm, sinks, w_pool_proj, w_attn_proj, w_out, norm_ffn2, ffn2_w_in, ffn2_w_out):
    depth = norm_ffn1.shape[0]
    assert depth == 1, "single-layer trunk"
    bp, sp, _ = x_prompt.shape
    bs, ts, _ = x_sample.shape

    ffn1 = _ffn_weights(norm_ffn1[0], ffn1_w_in[0], ffn1_w_out[0])
    ffn2 = _ffn_weights(norm_ffn2[0], ffn2_w_in[0], ffn2_w_out[0])
    win = w_in[0].astype(BF16)
    front_w = ffn1 + (norm_mix[0].reshape(1, D_MODEL), win[:, :D_UQKV],
                      jnp.tile(q_norm[0], 2).reshape(1, LANES), jnp.tile(k_norm[0], 2).reshape(1, LANES))
    mix_w = (pool_w[0].astype(BF16), pool_b[0].reshape(1, D_POOL), pool_scale[0].reshape(1, D_POOL),
             win[:, D_UQKV:], b_gate[0].reshape(1, 2 * D_MODEL),
             w_pool_proj[0].astype(BF16), w_attn_proj[0].astype(BF16), w_out[0].astype(BF16))
    sink_vec = sinks[0]

    x1, h, u, q, kv = _front_call(x_prompt.reshape(bp * sp, D_MODEL), *front_w)
    shp = lambda a: a.reshape(bp, sp, a.shape[-1])
    y_prompt = _prompt_mix_call(sink_vec, shp(x1), shp(h), shp(u), shp(q), shp(kv), mix_w, ffn2)
    u3, kv3 = shp(u), shp(kv)
    new_pool_p = u3[:, sp - POOL_HIST:]
    new_k_p = kv3[:, sp - WINDOW:, :D_KV].reshape(bp, WINDOW, N_KV_HEADS, HEAD_DIM)
    new_v_p = kv3[:, sp - WINDOW:, D_KV:].reshape(bp, WINDOW, N_KV_HEADS, HEAD_DIM)

    x1s, hs, us, qs, kvs = _front_call(x_sample.reshape(bs * ts, D_MODEL), *front_w)
    shs = lambda a: a.reshape(bs, ts, a.shape[-1])
    hist_u = jnp.pad(state_pool[0], ((0, 0), (POOL_HIST_PAD - POOL_HIST, 0), (0, 0)))
    hist_kv = jnp.concatenate([cache_k[0].reshape(bs, WINDOW, D_KV), cache_v[0].reshape(bs, WINDOW, D_KV)], axis=-1)
    x2s = _sample_mix_call(sink_vec, shs(x1s), shs(hs), shs(us), shs(qs), shs(kvs), hist_u, hist_kv, mix_w)
    y_sample = _ffn_call(x2s.reshape(bs * ts, D_MODEL), *ffn2).reshape(bs, ts, D_MODEL)
    us3, kvs3 = shs(us), shs(kvs)
    new_pool_s = jnp.concatenate([state_pool[0], us3], axis=1)[:, -POOL_HIST:]
    kv_full = jnp.concatenate([hist_kv, kvs3], axis=1)[:, -WINDOW:]
    new_k_s = kv_full[..., :D_KV].reshape(bs, WINDOW, N_KV_HEADS, HEAD_DIM)
    new_v_s = kv_full[..., D_KV:].reshape(bs, WINDOW, N_KV_HEADS, HEAD_DIM)

    return (y_prompt, y_sample, new_pool_p[None], new_k_p[None], new_v_p[None],
            new_pool_s[None], new_k_s[None], new_v_s[None])
```

```python
import functools

import jax
import jax.numpy as jnp
from jax import lax
from jax.experimental import pallas as pl
from jax.experimental.pallas import tpu as pltpu

D_MODEL = 1024
CHUNK = 64
N_HEADS = 16
N_KV_HEADS = 2
HEAD_DIM = 64
WINDOW = 128
D_POOL = 512
N_POOL_GROUPS = 4
POOL_GROUP = 128
POOL_WINDOWS = (2, 4, 8, 16)
POOL_HIST = 15
POOL_HIST_PAD = 16
D_Q = N_HEADS * HEAD_DIM
D_KV = N_KV_HEADS * HEAD_DIM
D_UQKV = D_POOL + D_Q + 2 * D_KV
D_FF = 2816
EPS = 1e-6

LANES = 128
FF_CHUNK = 256
N_FF_CHUNKS = D_FF // FF_CHUNK
PAIRS_PER_KV = (N_HEADS // N_KV_HEADS) // 2
VMEM_LIMIT_BYTES = 56 * 1024 * 1024
NEG_BIG = float(jnp.finfo(jnp.float32).min)

BF16 = jnp.bfloat16
F32 = jnp.float32


def _dot(a, b):
    return jnp.dot(a, b, preferred_element_type=F32)


def _dot_nt(a, b):
    return lax.dot_general(a, b, (((1,), (1,)), ((), ())), preferred_element_type=F32)


def _rmsnorm(x, g):
    return x * lax.rsqrt(jnp.mean(x * x, axis=-1, keepdims=True) + EPS) * g


def _ffn(x, g_ref, w1_ref, w2_ref, a_scr):
    h = _rmsnorm(x, g_ref[...]).astype(BF16)
    for c in range(N_FF_CHUNKS):
        gate = _dot(h, w1_ref[:, c * FF_CHUNK:(c + 1) * FF_CHUNK])
        up = _dot(h, w1_ref[:, D_FF + c * FF_CHUNK:D_FF + (c + 1) * FF_CHUNK])
        a_scr[:, c * FF_CHUNK:(c + 1) * FF_CHUNK] = (gate * jax.nn.sigmoid(gate) * up).astype(BF16)
    return x + 0.5 * _dot(a_scr[...], w2_ref[...])


def _half_sumsq(x):
    y = x * x
    low = lax.broadcasted_iota(jnp.int32, y.shape, 1) < HEAD_DIM
    sum_low = jnp.sum(jnp.where(low, y, 0.0), axis=-1, keepdims=True)
    sum_high = jnp.sum(jnp.where(low, 0.0, y), axis=-1, keepdims=True)
    return jnp.where(low, sum_low, sum_high)


def _head_norm(x, g2):
    return x * lax.rsqrt(_half_sumsq(x) * (1.0 / HEAD_DIM) + EPS) * g2


def _front_body(x_ref, g1_ref, w1_ref, w2_ref, gm_ref, win_ref, qn_ref, kn_ref,
                x1_ref, h_ref, u_ref, q_ref, kv_ref, a_scr):
    x1 = _ffn(x_ref[...], g1_ref, w1_ref, w2_ref, a_scr)
    x1_ref[...] = x1
    h = _rmsnorm(x1, gm_ref[...]).astype(BF16)
    h_ref[...] = h
    z = _dot(h, win_ref[...])
    u_ref[...] = z[:, :D_POOL]
    qg = qn_ref[...] * (HEAD_DIM ** -0.5)
    for p in range(D_Q // LANES):
        lo = D_POOL + p * LANES
        q_ref[:, p * LANES:(p + 1) * LANES] = _head_norm(z[:, lo:lo + LANES], qg).astype(BF16)
    k0 = D_POOL + D_Q
    kv_ref[:, :D_KV] = _head_norm(z[:, k0:k0 + D_KV], kn_ref[...])
    kv_ref[:, D_KV:] = z[:, k0 + D_KV:k0 + 2 * D_KV]


def _ffn_body(x_ref, g_ref, w1_ref, w2_ref, o_ref, a_scr):
    o_ref[...] = _ffn(x_ref[...], g_ref, w1_ref, w2_ref, a_scr)


def _pool_mixer(uext_ref, rows, first_row, pw_ref, pb_ref, ps_ref):
    u_new = uext_ref[POOL_HIST_PAD:, :]
    outs = []
    for gi, w in enumerate(POOL_WINDOWS):
        sl = slice(gi * POOL_GROUP, (gi + 1) * POOL_GROUP)
        s = u_new[:, sl]
        for i in range(1, w):
            s = s + uext_ref[POOL_HIST_PAD - i:POOL_HIST_PAD - i + rows, sl]
        if first_row is None:
            mean = s * (1.0 / w)
        else:
            pos = first_row + lax.broadcasted_iota(jnp.int32, (rows, 1), 0)
            mean = s / jnp.minimum(pos + 1, w).astype(F32)
        pooled = (mean - u_new[:, sl]).astype(BF16)
        y = _dot(pooled, pw_ref[gi]) + pb_ref[:, sl]
        outs.append(y * ps_ref[:, sl])
    return jnp.concatenate(outs, axis=-1)


def _fill_kv_variants(kvx_ref, row0, kv):
    rows = kv.shape[0]
    lane = lax.broadcasted_iota(jnp.int32, (rows, LANES), 1)
    low = lane < HEAD_DIM
    for t, base in ((0, 0), (1, D_KV)):
        both = kv[:, base:base + D_KV]
        swapped = pltpu.roll(both, HEAD_DIM, axis=1)
        variants = (jnp.where(low, both, 0.0), jnp.where(low, 0.0, swapped),
                    jnp.where(low, swapped, 0.0), jnp.where(low, 0.0, both))
        for i, val in enumerate(variants):
            kvx_ref[4 * t + i, row0:row0 + rows, :] = val.astype(BF16)


def _attend_block(q_ref, q_row0, q_rows, kvx_ref, key_row0, n_keys, key_valid, sinks_ref, yb_ref):
    m_rows = PAIRS_PER_KV * q_rows
    row_blk = lax.broadcasted_iota(jnp.int32, (m_rows, 1), 0) // q_rows
    lane = lax.broadcasted_iota(jnp.int32, (m_rows, LANES), 1)
    for kh in range(N_KV_HEADS):
        qs = jnp.concatenate(
            [q_ref[q_row0:q_row0 + q_rows, (kh * PAIRS_PER_KV + pp) * LANES:(kh * PAIRS_PER_KV + pp + 1) * LANES]
             for pp in range(PAIRS_PER_KV)], axis=0)
        o = None
        inv = []
        for ab in range(2):
            sink = jnp.zeros((m_rows, 1), F32)
            for pp in range(PAIRS_PER_KV):
                sink = jnp.where(row_blk == pp, sinks_ref[kh * 2 * PAIRS_PER_KV + 2 * pp + ab], sink)
            k_op = kvx_ref[2 * kh + ab, key_row0:key_row0 + n_keys, :]
            v_op = kvx_ref[4 + 2 * kh + ab, key_row0:key_row0 + n_keys, :]
            s = _dot_nt(qs, k_op)
            if key_valid is not None:
                s = jnp.where(key_valid, s, NEG_BIG)
            m = jnp.maximum(jnp.max(s, axis=-1, keepdims=True), sink)
            p = jnp.exp(s - m)
            denom = jnp.sum(p, axis=-1, keepdims=True) + jnp.exp(sink - m)
            inv.append(1.0 / denom)
            pv = _dot(p.astype(BF16), v_op)
            o = pv if o is None else o + pv
        o = o * jnp.where(lane < HEAD_DIM, inv[0], inv[1])
        for pp in range(PAIRS_PER_KV):
            col = (kh * PAIRS_PER_KV + pp) * LANES
            yb_ref[q_row0:q_row0 + q_rows, col:col + LANES] = o[pp * q_rows:(pp + 1) * q_rows].astype(BF16)


def _mix_out(x1, h, ya, yb, wg_ref, bg_ref, wpp_ref, wap_ref, wo_ref):
    gates = jax.nn.sigmoid(_dot(h, wg_ref[...]) + bg_ref[...])
    a = _dot(ya.astype(BF16), wpp_ref[...])
    b = _dot(yb, wap_ref[...])
    mixed = gates[:, :D_MODEL] * a + gates[:, D_MODEL:] * b
    return x1 + _dot(mixed.astype(BF16), wo_ref[...])


def _prompt_mix_body(sinks_ref, x1_ref, h_ref, u_ref, q_ref, kv_ref,
                     pw_ref, pb_ref, ps_ref, wg_ref, bg_ref, wpp_ref, wap_ref, wo_ref,
                     g2_ref, w1_ref, w2_ref,
                     o_ref, uext, kvx, yb_scr, a_scr, *, rows):
    t = pl.program_id(1)

    @pl.when(t == 0)
    def _():
        uext[:POOL_HIST_PAD, :] = jnp.zeros((POOL_HIST_PAD, D_POOL), F32)
        kvx[:, :WINDOW, :] = jnp.zeros((8, WINDOW, LANES), BF16)

    uext[POOL_HIST_PAD:, :] = u_ref[...]
    _fill_kv_variants(kvx, WINDOW, kv_ref[...])
    ya = _pool_mixer(uext, rows, t * rows, pw_ref, pb_ref, ps_ref)

    n_keys = WINDOW + CHUNK
    key_chunk = lax.broadcasted_iota(jnp.int32, (1, n_keys), 1) // CHUNK
    for j in range(rows // CHUNK):
        first_chunk = t * (rows // CHUNK) + j - WINDOW // CHUNK
        _attend_block(q_ref, j * CHUNK, CHUNK, kvx, j * CHUNK, n_keys, key_chunk + first_chunk >= 0,
                      sinks_ref, yb_scr)

    x2 = _mix_out(x1_ref[...], h_ref[...], ya, yb_scr[...], wg_ref, bg_ref, wpp_ref, wap_ref, wo_ref)
    o_ref[...] = _ffn(x2, g2_ref, w1_ref, w2_ref, a_scr)

    uext[:POOL_HIST_PAD, :] = uext[rows:rows + POOL_HIST_PAD, :]
    kvx[:, :WINDOW, :] = kvx[:, rows:rows + WINDOW, :]


def _sample_mix_body(sinks_ref, x1_ref, h_ref, u_ref, q_ref, kv_ref, hu_ref, hkv_ref,
                     pw_ref, pb_ref, ps_ref, wg_ref, bg_ref, wpp_ref, wap_ref, wo_ref,
                     o_ref, uext, kvx, yb_scr, *, rows):
    uext[:POOL_HIST_PAD, :] = hu_ref[...]
    uext[POOL_HIST_PAD:, :] = u_ref[...]
    _fill_kv_variants(kvx, 0, hkv_ref[...])
    _fill_kv_variants(kvx, WINDOW, kv_ref[...])
    ya = _pool_mixer(uext, rows, None, pw_ref, pb_ref, ps_ref)
    _attend_block(q_ref, 0, rows, kvx, 0, WINDOW + rows, None, sinks_ref, yb_scr)
    o_ref[...] = _mix_out(x1_ref[...], h_ref[...], ya, yb_scr[...], wg_ref, bg_ref, wpp_ref, wap_ref, wo_ref)


def _const_spec(shape):
    return pl.BlockSpec(shape, lambda *_: (0,) * len(shape), pipeline_mode=pl.Buffered(1))


def _row_tile(n_rows, target=256):
    return target if n_rows % target == 0 else n_rows


def _front_call(x2d, g1, w1, w2, gm, win, qn2, kn2):
    n = x2d.shape[0]
    tm = _row_tile(n, 512)
    row = lambda width: pl.BlockSpec((tm, width), lambda i: (i, 0))
    return pl.pallas_call(
        _front_body,
        grid=(n // tm,),
        in_specs=[row(D_MODEL), _const_spec(g1.shape), _const_spec(w1.shape), _const_spec(w2.shape),
                  _const_spec(gm.shape), _const_spec(win.shape), _const_spec(qn2.shape), _const_spec(kn2.shape)],
        out_specs=[row(D_MODEL), row(D_MODEL), row(D_POOL), row(D_Q), row(2 * D_KV)],
        out_shape=[jax.ShapeDtypeStruct((n, D_MODEL), F32), jax.ShapeDtypeStruct((n, D_MODEL), BF16),
                   jax.ShapeDtypeStruct((n, D_POOL), F32), jax.ShapeDtypeStruct((n, D_Q), BF16),
                   jax.ShapeDtypeStruct((n, 2 * D_KV), F32)],
        scratch_shapes=[pltpu.VMEM((tm, D_FF), BF16)],
        compiler_params=pltpu.CompilerParams(dimension_semantics=("parallel",),
                                             vmem_limit_bytes=VMEM_LIMIT_BYTES),
        name="front",
    )(x2d, g1, w1, w2, gm, win, qn2, kn2)


def _ffn_call(x2d, g, w1, w2):
    n = x2d.shape[0]
    tm = _row_tile(n)
    row = pl.BlockSpec((tm, D_MODEL), lambda i: (i, 0))
    return pl.pallas_call(
        _ffn_body,
        grid=(n // tm,),
        in_specs=[row, _const_spec(g.shape), _const_spec(w1.shape), _const_spec(w2.shape)],
        out_specs=row,
        out_shape=jax.ShapeDtypeStruct((n, D_MODEL), F32),
        scratch_shapes=[pltpu.VMEM((tm, D_FF), BF16)],
        compiler_params=pltpu.CompilerParams(dimension_semantics=("parallel",),
                                             vmem_limit_bytes=VMEM_LIMIT_BYTES),
        name="ffn",
    )(x2d, g, w1, w2)


def _prompt_mix_call(sinks, x1, h, u, q, kv, mix_w, ffn_w):
    b, s, _ = x1.shape
    tm = _row_tile(s)
    row = lambda width: pl.BlockSpec((None, tm, width), lambda i, t: (i, t, 0))
    weights = tuple(mix_w) + tuple(ffn_w)
    return pl.pallas_call(
        functools.partial(_prompt_mix_body, rows=tm),
        grid=(b, s // tm),
        in_specs=[pl.BlockSpec(memory_space=pltpu.SMEM),
                  row(D_MODEL), row(D_MODEL), row(D_POOL), row(D_Q), row(2 * D_KV)]
                 + [_const_spec(w.shape) for w in weights],
        out_specs=row(D_MODEL),
        out_shape=jax.ShapeDtypeStruct((b, s, D_MODEL), F32),
        scratch_shapes=[pltpu.VMEM((POOL_HIST_PAD + tm, D_POOL), F32),
                        pltpu.VMEM((8, WINDOW + tm, LANES), BF16),
                        pltpu.VMEM((tm, D_Q), BF16),
                        pltpu.VMEM((tm, D_FF), BF16)],
        compiler_params=pltpu.CompilerParams(dimension_semantics=("arbitrary", "arbitrary"),
                                             vmem_limit_bytes=VMEM_LIMIT_BYTES),
        name="prompt_mix",
    )(sinks, x1, h, u, q, kv, *weights)


def _sample_mix_call(sinks, x1, h, u, q, kv, hist_u, hist_kv, mix_w):
    b, t, _ = x1.shape
    row = lambda rows, width: pl.BlockSpec((None, rows, width), lambda i: (i, 0, 0))
    return pl.pallas_call(
        functools.partial(_sample_mix_body, rows=t),
        grid=(b,),
        in_specs=[pl.BlockSpec(memory_space=pltpu.SMEM),
                  row(t, D_MODEL), row(t, D_MODEL), row(t, D_POOL), row(t, D_Q), row(t, 2 * D_KV),
                  row(POOL_HIST_PAD, D_POOL), row(WINDOW, 2 * D_KV)]
                 + [_const_spec(w.shape) for w in mix_w],
        out_specs=row(t, D_MODEL),
        out_shape=jax.ShapeDtypeStruct((b, t, D_MODEL), F32),
        scratch_shapes=[pltpu.VMEM((POOL_HIST_PAD + t, D_POOL), F32),
                        pltpu.VMEM((8, WINDOW + t, LANES), BF16),
                        pltpu.VMEM((t, D_Q), BF16)],
        compiler_params=pltpu.CompilerParams(dimension_semantics=("arbitrary",),
                                             vmem_limit_bytes=VMEM_LIMIT_BYTES),
        name="sample_mix",
    )(sinks, x1, h, u, q, kv, hist_u, hist_kv, *mix_w)


def _ffn_weights(norm, w_in, w_out):
    return norm.reshape(1, D_MODEL), w_in.astype(BF16), w_out.astype(BF16)


def kernel(x_prompt, x_sample, state_pool, cache_k, cache_v, norm_ffn1, ffn1_w_in, ffn1_w_out, norm_mix, w_in, b_gate, pool_w, pool_b, pool_scale, q_norm, k_norm, sinks, w_pool_proj, w_attn_proj, w_out, norm_ffn2, ffn2_w_in, ffn2_w_out):
    depth = norm_ffn1.shape[0]
    assert depth == 1, "single-layer trunk"
    bp, sp, _ = x_prompt.shape
    bs, ts, _ = x_sample.shape

    ffn1 = _ffn_weights(norm_ffn1[0], ffn1_w_in[0], ffn1_w_out[0])
    ffn2 = _ffn_weights(norm_ffn2[0], ffn2_w_in[0], ffn2_w_out[0])
    win = w_in[0].astype(BF16)
    front_w = ffn1 + (norm_mix[0].reshape(1, D_MODEL), win[:, :D_UQKV],
                      jnp.tile(q_norm[0], 2).reshape(1, LANES), jnp.tile(k_norm[0], 2).reshape(1, LANES))
    mix_w = (pool_w[0].astype(BF16), pool_b[0].reshape(1, D_POOL), pool_scale[0].reshape(1, D_POOL),
             win[:, D_UQKV:], b_gate[0].reshape(1, 2 * D_MODEL),
             w_pool_proj[0].astype(BF16), w_attn_proj[0].astype(BF16), w_out[0].astype(BF16))
    sink_vec = sinks[0]

    x1, h, u, q, kv = _front_call(x_prompt.reshape(bp * sp, D_MODEL), *front_w)
    shp = lambda a: a.reshape(bp, sp, a.shape[-1])
    y_prompt = _prompt_mix_call(sink_vec, shp(x1), shp(h), shp(u), shp(q), shp(kv), mix_w, ffn2)
    u3, kv3 = shp(u), shp(kv)
    new_pool_p = u3[:, sp - POOL_HIST:]
    new_k_p = kv3[:, sp - WINDOW:, :D_KV].reshape(bp, WINDOW, N_KV_HEADS, HEAD_DIM)
    new_v_p = kv3[:, sp - WINDOW:, D_KV:].reshape(bp, WINDOW, N_KV_HEADS, HEAD_DIM)

    x1s, hs, us, qs, kvs = _front_call(x_sample.reshape(bs * ts, D_MODEL), *front_w)
    shs = lambda a: a.reshape(bs, ts, a.shape[-1])
    hist_u = jnp.pad(state_pool[0], ((0, 0), (POOL_HIST_PAD - POOL_HIST, 0), (0, 0)))
    hist_kv = jnp.concatenate([cache_k[0].reshape(bs, WINDOW, D_KV), cache_v[0].reshape(bs, WINDOW, D_KV)], axis=-1)
    x2s = _sample_mix_call(sink_vec, shs(x1s), shs(hs), shs(us), shs(qs), shs(kvs), hist_u, hist_kv, mix_w)
    y_sample = _ffn_call(x2s.reshape(bs * ts, D_MODEL), *ffn2).reshape(bs, ts, D_MODEL)
    us3, kvs3 = shs(us), shs(kvs)
    new_pool_s = jnp.concatenate([state_pool[0], us3], axis=1)[:, -POOL_HIST:]
    kv_full = jnp.concatenate([hist_kv, kvs3], axis=1)[:, -WINDOW:]
    new_k_s = kv_full[..., :D_KV].reshape(bs, WINDOW, N_KV_HEADS, HEAD_DIM)
    new_v_s = kv_full[..., D_KV:].reshape(bs, WINDOW, N_KV_HEADS, HEAD_DIM)

    return (y_prompt, y_sample, new_pool_p[None], new_k_p[None], new_v_p[None],
            new_pool_s[None], new_k_s[None], new_v_s[None])
```

```python
import functools

import jax
import jax.numpy as jnp
from jax import lax
from jax.experimental import pallas as pl
from jax.experimental.pallas import tpu as pltpu

D_MODEL = 1024
CHUNK = 64
N_HEADS = 16
N_KV_HEADS = 2
HEAD_DIM = 64
WINDOW = 128
D_POOL = 512
N_POOL_GROUPS = 4
POOL_GROUP = 128
POOL_WINDOWS = (2, 4, 8, 16)
POOL_HIST = 15
POOL_HIST_PAD = 16
D_Q = N_HEADS * HEAD_DIM
D_KV = N_KV_HEADS * HEAD_DIM
D_UQKV = D_POOL + D_Q + 2 * D_KV
D_FF = 2816
EPS = 1e-6

LANES = 128
FF_CHUNK = 256
N_FF_CHUNKS = D_FF // FF_CHUNK
PAIRS_PER_KV = (N_HEADS // N_KV_HEADS) // 2
VMEM_LIMIT_BYTES = 56 * 1024 * 1024
NEG_BIG = float(jnp.finfo(jnp.float32).min)
LOG2E = 1.4426950408889634

BF16 = jnp.bfloat16
F32 = jnp.float32


def _dot(a, b):
    return jnp.dot(a, b, preferred_element_type=F32)


def _dot_nt(a, b):
    return lax.dot_general(a, b, (((1,), (1,)), ((), ())), preferred_element_type=F32)


def _rmsnorm(x, g):
    return x * lax.rsqrt(jnp.mean(x * x, axis=-1, keepdims=True) + EPS) * g


def _ffn(x, g_ref, w1_ref, w2_ref, a_scr):
    h = _rmsnorm(x, g_ref[...]).astype(BF16)
    for c in range(N_FF_CHUNKS):
        gate = _dot(h, w1_ref[:, c * FF_CHUNK:(c + 1) * FF_CHUNK])
        up = _dot(h, w1_ref[:, D_FF + c * FF_CHUNK:D_FF + (c + 1) * FF_CHUNK])
        a_scr[:, c * FF_CHUNK:(c + 1) * FF_CHUNK] = (gate * jax.nn.sigmoid(gate) * up).astype(BF16)
    return x + 0.5 * _dot(a_scr[...], w2_ref[...])


def _half_sumsq(x):
    y = x * x
    low = lax.broadcasted_iota(jnp.int32, y.shape, 1) < HEAD_DIM
    sum_low = jnp.sum(jnp.where(low, y, 0.0), axis=-1, keepdims=True)
    sum_high = jnp.sum(jnp.where(low, 0.0, y), axis=-1, keepdims=True)
    return jnp.where(low, sum_low, sum_high)


def _head_norm(x, g2):
    return x * lax.rsqrt(_half_sumsq(x) * (1.0 / HEAD_DIM) + EPS) * g2


def _front_body(x_ref, g1_ref, w1_ref, w2_ref, gm_ref, win_ref, qn_ref, kn_ref,
                x1_ref, h_ref, u_ref, q_ref, kv_ref, a_scr):
    x1 = _ffn(x_ref[...], g1_ref, w1_ref, w2_ref, a_scr)
    x1_ref[...] = x1
    h = _rmsnorm(x1, gm_ref[...]).astype(BF16)
    h_ref[...] = h
    z = _dot(h, win_ref[...])
    u_ref[...] = z[:, :D_POOL]
    qg = qn_ref[...] * (HEAD_DIM ** -0.5 * LOG2E)
    for p in range(D_Q // LANES):
        lo = D_POOL + p * LANES
        q_ref[:, p * LANES:(p + 1) * LANES] = _head_norm(z[:, lo:lo + LANES], qg).astype(BF16)
    k0 = D_POOL + D_Q
    kv_ref[:, :D_KV] = _head_norm(z[:, k0:k0 + D_KV], kn_ref[...])
    kv_ref[:, D_KV:] = z[:, k0 + D_KV:k0 + 2 * D_KV]


def _ffn_body(x_ref, g_ref, w1_ref, w2_ref, o_ref, a_scr):
    o_ref[...] = _ffn(x_ref[...], g_ref, w1_ref, w2_ref, a_scr)


def _pool_mixer(uext_ref, rows, first_row, pw_ref, pb_ref, ps_ref):
    u_new = uext_ref[POOL_HIST_PAD:, :]
    outs = []
    for gi, w in enumerate(POOL_WINDOWS):
        sl = slice(gi * POOL_GROUP, (gi + 1) * POOL_GROUP)
        s = u_new[:, sl]
        for i in range(1, w):
            s = s + uext_ref[POOL_HIST_PAD - i:POOL_HIST_PAD - i + rows, sl]
        if first_row is None:
            mean = s * (1.0 / w)
        else:
            pos = first_row + lax.broadcasted_iota(jnp.int32, (rows, 1), 0)
            mean = s / jnp.minimum(pos + 1, w).astype(F32)
        pooled = (mean - u_new[:, sl]).astype(BF16)
        y = _dot(pooled, pw_ref[gi]) + pb_ref[:, sl]
        outs.append(y * ps_ref[:, sl])
    return jnp.concatenate(outs, axis=-1)


def _fill_kv_variants(kx_ref, vx_ref, row0, kv):
    rows = kv.shape[0]
    low = lax.broadcasted_iota(jnp.int32, (rows, LANES), 1) < HEAD_DIM
    ones = (jnp.where(low, 1.0, 0.0).astype(BF16), jnp.where(low, 0.0, 1.0).astype(BF16))
    for t, base in ((0, 0), (1, D_KV)):
        both = kv[:, base:base + D_KV]
        swapped = pltpu.roll(both, HEAD_DIM, axis=1)
        variants = (jnp.where(low, both, 0.0), jnp.where(low, 0.0, swapped),
                    jnp.where(low, swapped, 0.0), jnp.where(low, 0.0, both))
        for i, val in enumerate(variants):
            if t == 0:
                kx_ref[i, row0:row0 + rows, :] = val.astype(BF16)
            else:
                vx_ref[i, row0:row0 + rows, :LANES] = val.astype(BF16)
                vx_ref[i, row0:row0 + rows, LANES:] = ones[i % 2]


def _sink_columns(sinks_ref, q_rows):
    m_rows = PAIRS_PER_KV * q_rows
    row_blk = lax.broadcasted_iota(jnp.int32, (m_rows, 1), 0) // q_rows
    cols = []
    for kh in range(N_KV_HEADS):
        pair = []
        for ab in range(2):
            sink = jnp.zeros((m_rows, 1), F32)
            for pp in range(PAIRS_PER_KV):
                sink = jnp.where(row_blk == pp, sinks_ref[kh * 2 * PAIRS_PER_KV + 2 * pp + ab] * LOG2E, sink)
            pair.append(sink)
        cols.append(pair)
    return cols


def _attend_block(q_ref, q_row0, q_rows, kx_ref, vx_ref, key_row0, n_keys, key_valid, sink_cols, yb_ref):
    m_rows = PAIRS_PER_KV * q_rows
    low = lax.broadcasted_iota(jnp.int32, (m_rows, LANES), 1) < HEAD_DIM
    for kh in range(N_KV_HEADS):
        qs = jnp.concatenate(
            [q_ref[q_row0:q_row0 + q_rows, (kh * PAIRS_PER_KV + pp) * LANES:(kh * PAIRS_PER_KV + pp + 1) * LANES]
             for pp in range(PAIRS_PER_KV)], axis=0)
        o = None
        row_max = []
        for ab in range(2):
            s = _dot_nt(qs, kx_ref[2 * kh + ab, key_row0:key_row0 + n_keys, :])
            if key_valid is not None:
                s = jnp.where(key_valid, s, NEG_BIG)
            m = jnp.maximum(jnp.max(s, axis=-1, keepdims=True), sink_cols[kh][ab])
            p = jnp.exp2(s - m).astype(BF16)
            pv = _dot(p, vx_ref[2 * kh + ab, key_row0:key_row0 + n_keys, :])
            o = pv if o is None else o + pv
            row_max.append(m)
        sink_p = jnp.exp2(jnp.where(low, sink_cols[kh][0], sink_cols[kh][1])
                          - jnp.where(low, row_max[0], row_max[1]))
        out = o[:, :LANES] / (o[:, LANES:] + sink_p)
        for pp in range(PAIRS_PER_KV):
            col = (kh * PAIRS_PER_KV + pp) * LANES
            yb_ref[q_row0:q_row0 + q_rows, col:col + LANES] = out[pp * q_rows:(pp + 1) * q_rows].astype(BF16)


def _mix_out(x1, h, ya, yb, wg_ref, bg_ref, wpp_ref, wap_ref, wo_ref):
    gates = jax.nn.sigmoid(_dot(h, wg_ref[...]) + bg_ref[...])
    a = _dot(ya.astype(BF16), wpp_ref[...])
    b = _dot(yb, wap_ref[...])
    mixed = gates[:, :D_MODEL] * a + gates[:, D_MODEL:] * b
    return x1 + _dot(mixed.astype(BF16), wo_ref[...])


def _prompt_mix_body(sinks_ref, x1_ref, h_ref, u_ref, q_ref, kv_ref,
                     pw_ref, pb_ref, ps_ref, wg_ref, bg_ref, wpp_ref, wap_ref, wo_ref,
                     g2_ref, w1_ref, w2_ref,
                     o_ref, uext, kx, vx, yb_scr, a_scr, *, rows):
    t = pl.program_id(1)

    @pl.when(t == 0)
    def _():
        uext[:POOL_HIST_PAD, :] = jnp.zeros((POOL_HIST_PAD, D_POOL), F32)
        kx[:, :WINDOW, :] = jnp.zeros((4, WINDOW, LANES), BF16)
        vx[:, :WINDOW, :] = jnp.zeros((4, WINDOW, 2 * LANES), BF16)

    uext[POOL_HIST_PAD:, :] = u_ref[...]
    _fill_kv_variants(kx, vx, WINDOW, kv_ref[...])
    ya = _pool_mixer(uext, rows, t * rows, pw_ref, pb_ref, ps_ref)

    n_keys = WINDOW + CHUNK
    key_chunk = lax.broadcasted_iota(jnp.int32, (1, n_keys), 1) // CHUNK
    sink_cols = _sink_columns(sinks_ref, CHUNK)
    for j in range(rows // CHUNK):
        first_chunk = t * (rows // CHUNK) + j - WINDOW // CHUNK
        _attend_block(q_ref, j * CHUNK, CHUNK, kx, vx, j * CHUNK, n_keys, key_chunk + first_chunk >= 0,
                      sink_cols, yb_scr)

    x2 = _mix_out(x1_ref[...], h_ref[...], ya, yb_scr[...], wg_ref, bg_ref, wpp_ref, wap_ref, wo_ref)
    o_ref[...] = _ffn(x2, g2_ref, w1_ref, w2_ref, a_scr)

    uext[:POOL_HIST_PAD, :] = uext[rows:rows + POOL_HIST_PAD, :]
    kx[:, :WINDOW, :] = kx[:, rows:rows + WINDOW, :]
    vx[:, :WINDOW, :] = vx[:, rows:rows + WINDOW, :]


def _sample_mix_body(sinks_ref, x1_ref, h_ref, u_ref, q_ref, kv_ref, hu_ref, hkv_ref,
                     pw_ref, pb_ref, ps_ref, wg_ref, bg_ref, wpp_ref, wap_ref, wo_ref,
                     o_ref, uext, kx, vx, yb_scr, *, rows):
    uext[:POOL_HIST_PAD, :] = hu_ref[...]
    uext[POOL_HIST_PAD:, :] = u_ref[...]
    _fill_kv_variants(kx, vx, 0, hkv_ref[...])
    _fill_kv_variants(kx, vx, WINDOW, kv_ref[...])
    ya = _pool_mixer(uext, rows, None, pw_ref, pb_ref, ps_ref)
    _attend_block(q_ref, 0, rows, kx, vx, 0, WINDOW + rows, None, _sink_columns(sinks_ref, rows), yb_scr)
    o_ref[...] = _mix_out(x1_ref[...], h_ref[...], ya, yb_scr[...], wg_ref, bg_ref, wpp_ref, wap_ref, wo_ref)


def _const_spec(shape):
    return pl.BlockSpec(shape, lambda *_: (0,) * len(shape), pipeline_mode=pl.Buffered(1))


def _row_tile(n_rows, target=256):
    return target if n_rows % target == 0 else n_rows


def _front_call(x2d, g1, w1, w2, gm, win, qn2, kn2):
    n = x2d.shape[0]
    tm = _row_tile(n, 512)
    row = lambda width: pl.BlockSpec((tm, width), lambda i: (i, 0))
    return pl.pallas_call(
        _front_body,
        grid=(n // tm,),
        in_specs=[row(D_MODEL), _const_spec(g1.shape), _const_spec(w1.shape), _const_spec(w2.shape),
                  _const_spec(gm.shape), _const_spec(win.shape), _const_spec(qn2.shape), _const_spec(kn2.shape)],
        out_specs=[row(D_MODEL), row(D_MODEL), row(D_POOL), row(D_Q), row(2 * D_KV)],
        out_shape=[jax.ShapeDtypeStruct((n, D_MODEL), F32), jax.ShapeDtypeStruct((n, D_MODEL), BF16),
                   jax.ShapeDtypeStruct((n, D_POOL), F32), jax.ShapeDtypeStruct((n, D_Q), BF16),
                   jax.ShapeDtypeStruct((n, 2 * D_KV), F32)],
        scratch_shapes=[pltpu.VMEM((tm, D_FF), BF16)],
        compiler_params=pltpu.CompilerParams(dimension_semantics=("parallel",),
                                             vmem_limit_bytes=VMEM_LIMIT_BYTES),
        name="front",
    )(x2d, g1, w1, w2, gm, win, qn2, kn2)


def _ffn_call(x2d, g, w1, w2):
    n = x2d.shape[0]
    tm = _row_tile(n)
    row = pl.BlockSpec((tm, D_MODEL), lambda i: (i, 0))
    return pl.pallas_call(
        _ffn_body,
        grid=(n // tm,),
        in_specs=[row, _const_spec(g.shape), _const_spec(w1.shape), _const_spec(w2.shape)],
        out_specs=row,
        out_shape=jax.ShapeDtypeStruct((n, D_MODEL), F32),
        scratch_shapes=[pltpu.VMEM((tm, D_FF), BF16)],
        compiler_params=pltpu.CompilerParams(dimension_semantics=("parallel",),
                                             vmem_limit_bytes=VMEM_LIMIT_BYTES),
        name="ffn",
    )(x2d, g, w1, w2)


def _prompt_mix_call(sinks, x1, h, u, q, kv, mix_w, ffn_w):
    b, s, _ = x1.shape
    tm = _row_tile(s)
    row = lambda width: pl.BlockSpec((None, tm, width), lambda i, t: (i, t, 0))
    weights = tuple(mix_w) + tuple(ffn_w)
    return pl.pallas_call(
        functools.partial(_prompt_mix_body, rows=tm),
        grid=(b, s // tm),
        in_specs=[pl.BlockSpec(memory_space=pltpu.SMEM),
                  row(D_MODEL), row(D_MODEL), row(D_POOL), row(D_Q), row(2 * D_KV)]
                 + [_const_spec(w.shape) for w in weights],
        out_specs=row(D_MODEL),
        out_shape=jax.ShapeDtypeStruct((b, s, D_MODEL), F32),
        scratch_shapes=[pltpu.VMEM((POOL_HIST_PAD + tm, D_POOL), F32),
                        pltpu.VMEM((4, WINDOW + tm, LANES), BF16),
                        pltpu.VMEM((4, WINDOW + tm, 2 * LANES), BF16),
                        pltpu.VMEM((tm, D_Q), BF16),
                        pltpu.VMEM((tm, D_FF), BF16)],
        compiler_params=pltpu.CompilerParams(dimension_semantics=("arbitrary", "arbitrary"),
                                             vmem_limit_bytes=VMEM_LIMIT_BYTES),
        name="prompt_mix",
    )(sinks, x1, h, u, q, kv, *weights)


def _sample_mix_call(sinks, x1, h, u, q, kv, hist_u, hist_kv, mix_w):
    b, t, _ = x1.shape
    row = lambda rows, width: pl.BlockSpec((None, rows, width), lambda i: (i, 0, 0))
    return pl.pallas_call(
        functools.partial(_sample_mix_body, rows=t),
        grid=(b,),
        in_specs=[pl.BlockSpec(memory_space=pltpu.SMEM),
                  row(t, D_MODEL), row(t, D_MODEL), row(t, D_POOL), row(t, D_Q), row(t, 2 * D_KV),
                  row(POOL_HIST_PAD, D_POOL), row(WINDOW, 2 * D_KV)]
                 + [_const_spec(w.shape) for w in mix_w],
        out_specs=row(t, D_MODEL),
        out_shape=jax.ShapeDtypeStruct((b, t, D_MODEL), F32),
        scratch_shapes=[pltpu.VMEM((POOL_HIST_PAD + t, D_POOL), F32),
                        pltpu.VMEM((4, WINDOW + t, LANES), BF16),
                        pltpu.VMEM((4, WINDOW + t, 2 * LANES), BF16),
                        pltpu.VMEM((t, D_Q), BF16)],
        compiler_params=pltpu.CompilerParams(dimension_semantics=("arbitrary",),
                                             vmem_limit_bytes=VMEM_LIMIT_BYTES),
        name="sample_mix",
    )(sinks, x1, h, u, q, kv, hist_u, hist_kv, *mix_w)


def _ffn_weights(norm, w_in, w_out):
    return norm.reshape(1, D_MODEL), w_in.astype(BF16), w_out.astype(BF16)


def kernel(x_prompt, x_sample, state_pool, cache_k, cache_v, norm_ffn1, ffn1_w_in, ffn1_w_out, norm_mix, w_in, b_gate, pool_w, pool_b, pool_scale, q_norm, k_norm, sinks, w_pool_proj, w_attn_proj, w_out, norm_ffn2, ffn2_w_in, ffn2_w_out):
    depth = norm_ffn1.shape[0]
    assert depth == 1, "single-layer trunk"
    bp, sp, _ = x_prompt.shape
    bs, ts, _ = x_sample.shape

    ffn1 = _ffn_weights(norm_ffn1[0], ffn1_w_in[0], ffn1_w_out[0])
    ffn2 = _ffn_weights(norm_ffn2[0], ffn2_w_in[0], ffn2_w_out[0])
    win = w_in[0].astype(BF16)
    front_w = ffn1 + (norm_mix[0].reshape(1, D_MODEL), win[:, :D_UQKV],
                      jnp.tile(q_norm[0], 2).reshape(1, LANES), jnp.tile(k_norm[0], 2).reshape(1, LANES))
    mix_w = (pool_w[0].astype(BF16), pool_b[0].reshape(1, D_POOL), pool_scale[0].reshape(1, D_POOL),
             win[:, D_UQKV:], b_gate[0].reshape(1, 2 * D_MODEL),
             w_pool_proj[0].astype(BF16), w_attn_proj[0].astype(BF16), w_out[0].astype(BF16))
    sink_vec = sinks[0]

    x1, h, u, q, kv = _front_call(x_prompt.reshape(bp * sp, D_MODEL), *front_w)
    shp = lambda a: a.reshape(bp, sp, a.shape[-1])
    y_prompt = _prompt_mix_call(sink_vec, shp(x1), shp(h), shp(u), shp(q), shp(kv), mix_w, ffn2)
    u3, kv3 = shp(u), shp(kv)
    new_pool_p = u3[:, sp - POOL_HIST:]
    new_k_p = kv3[:, sp - WINDOW:, :D_KV].reshape(bp, WINDOW, N_KV_HEADS, HEAD_DIM)
    new_v_p = kv3[:, sp - WINDOW:, D_KV:].reshape(bp, WINDOW, N_KV_HEADS, HEAD_DIM)

    x1s, hs, us, qs, kvs = _front_call(x_sample.reshape(bs * ts, D_MODEL), *front_w)
    shs = lambda a: a.reshape(bs, ts, a.shape[-1])
    hist_u = jnp.pad(state_pool[0], ((0, 0), (POOL_HIST_PAD - POOL_HIST, 0), (0, 0)))
    hist_kv = jnp.concatenate([cache_k[0].reshape(bs, WINDOW, D_KV), cache_v[0].reshape(bs, WINDOW, D_KV)], axis=-1)
    x2s = _sample_mix_call(sink_vec, shs(x1s), shs(hs), shs(us), shs(qs), shs(kvs), hist_u, hist_kv, mix_w)
    y_sample = _ffn_call(x2s.reshape(bs * ts, D_MODEL), *ffn2).reshape(bs, ts, D_MODEL)
    us3, kvs3 = shs(us), shs(kvs)
    new_pool_s = jnp.concatenate([state_pool[0], us3], axis=1)[:, -POOL_HIST:]
    kv_full = jnp.concatenate([hist_kv, kvs3], axis=1)[:, -WINDOW:]
    new_k_s = kv_full[..., :D_KV].reshape(bs, WINDOW, N_KV_HEADS, HEAD_DIM)
    new_v_s = kv_full[..., D_KV:].reshape(bs, WINDOW, N_KV_HEADS, HEAD_DIM)

    return (y_prompt, y_sample, new_pool_p[None], new_k_p[None], new_v_p[None],
            new_pool_s[None], new_k_s[None], new_v_s[None])
```

```python
import functools

import jax
import jax.numpy as jnp
from jax import lax
from jax.experimental import pallas as pl
from jax.experimental.pallas import tpu as pltpu

D_MODEL = 1024
CHUNK = 64
N_HEADS = 16
N_KV_HEADS = 2
HEAD_DIM = 64
WINDOW = 128
D_POOL = 512
N_POOL_GROUPS = 4
POOL_GROUP = 128
POOL_WINDOWS = (2, 4, 8, 16)
POOL_HIST = 15
POOL_HIST_PAD = 16
D_Q = N_HEADS * HEAD_DIM
D_KV = N_KV_HEADS * HEAD_DIM
D_UQKV = D_POOL + D_Q + 2 * D_KV
D_FF = 2816
EPS = 1e-6

LANES = 128
FF_CHUNK = 256
N_FF_CHUNKS = D_FF // FF_CHUNK
PAIRS_PER_KV = (N_HEADS // N_KV_HEADS) // 2
VMEM_LIMIT_BYTES = 56 * 1024 * 1024
NEG_BIG = float(jnp.finfo(jnp.float32).min)
LOG2E = 1.4426950408889634

BF16 = jnp.bfloat16
F32 = jnp.float32


def _dot(a, b):
    return jnp.dot(a, b, preferred_element_type=F32)


def _dot_nt(a, b):
    return lax.dot_general(a, b, (((1,), (1,)), ((), ())), preferred_element_type=F32)


def _rmsnorm(x, g):
    return x * lax.rsqrt(jnp.mean(x * x, axis=-1, keepdims=True) + EPS) * g


def _ffn(x, g_ref, w1_ref, w2_ref, a_scr, side_work=()):
    h = _rmsnorm(x, g_ref[...]).astype(BF16)
    for c in range(N_FF_CHUNKS):
        gate = _dot(h, w1_ref[:, c * FF_CHUNK:(c + 1) * FF_CHUNK])
        up = _dot(h, w1_ref[:, D_FF + c * FF_CHUNK:D_FF + (c + 1) * FF_CHUNK])
        a_scr[:, c * FF_CHUNK:(c + 1) * FF_CHUNK] = (gate * jax.nn.sigmoid(gate) * up).astype(BF16)
        if c < len(side_work):
            side_work[c]()
    assert len(side_work) <= N_FF_CHUNKS
    return x + 0.5 * _dot(a_scr[...], w2_ref[...])


def _half_sumsq(x):
    y = x * x
    low = lax.broadcasted_iota(jnp.int32, y.shape, 1) < HEAD_DIM
    sum_low = jnp.sum(jnp.where(low, y, 0.0), axis=-1, keepdims=True)
    sum_high = jnp.sum(jnp.where(low, 0.0, y), axis=-1, keepdims=True)
    return jnp.where(low, sum_low, sum_high)


def _head_norm(x, g2):
    return x * lax.rsqrt(_half_sumsq(x) * (1.0 / HEAD_DIM) + EPS) * g2


def _cast_block(src_ref, dst_ref):
    dst_ref[...] = src_ref[:, src_ref.shape[1] - dst_ref.shape[1]:].astype(BF16)


def _front_body(*refs, n_cast):
    x_ref, g1_ref, w1_ref, w2_ref, gm_ref, win_ref, qn_ref, kn_ref = refs[:8]
    cast_src = refs[8:8 + n_cast]
    x1_ref, h_ref, u_ref, q_ref, kv_ref = refs[8 + n_cast:13 + n_cast]
    cast_dst = refs[13 + n_cast:13 + 2 * n_cast]
    a_scr = refs[13 + 2 * n_cast]
    casts = [functools.partial(_cast_block, src, dst) for src, dst in zip(cast_src, cast_dst)]
    x1 = _ffn(x_ref[...], g1_ref, w1_ref, w2_ref, a_scr, casts)
    x1_ref[...] = x1
    h = _rmsnorm(x1, gm_ref[...]).astype(BF16)
    h_ref[...] = h
    qg = qn_ref[...] * (HEAD_DIM ** -0.5 * LOG2E)
    blk = 2 * LANES
    for b in range(D_Q // blk):
        zq = _dot(h, win_ref[:, D_POOL + b * blk:D_POOL + (b + 1) * blk])
        for p in range(2):
            q_ref[:, b * blk + p * LANES:b * blk + (p + 1) * LANES] = _head_norm(
                zq[:, p * LANES:(p + 1) * LANES], qg).astype(BF16)
    zkv = _dot(h, win_ref[:, D_POOL + D_Q:D_POOL + D_Q + 2 * D_KV])
    kv_ref[:, :D_KV] = _head_norm(zkv[:, :D_KV], kn_ref[...])
    kv_ref[:, D_KV:] = zkv[:, D_KV:]
    u_ref[...] = _dot(h, win_ref[:, :D_POOL])


def _ffn_body(x_ref, g_ref, w1_ref, w2_ref, o_ref, a_scr):
    o_ref[...] = _ffn(x_ref[...], g_ref, w1_ref, w2_ref, a_scr)


def _pool_pair(uext_ref, rows, first_row, pw_ref, pb_ref, ps_ref, pair):
    pooled = []
    for gi in (2 * pair, 2 * pair + 1):
        w = POOL_WINDOWS[gi]
        sl = slice(gi * POOL_GROUP, (gi + 1) * POOL_GROUP)
        ext = uext_ref[:, sl]
        u_new = ext[POOL_HIST_PAD:]
        s = ext
        k = 1
        while k < w:
            s = s + pltpu.roll(s, k, axis=0)
            k *= 2
        s = s[POOL_HIST_PAD:]
        if first_row is None:
            mean = s * (1.0 / w)
        else:
            pos = first_row + lax.broadcasted_iota(jnp.int32, (rows, 1), 0)
            mean = s / jnp.minimum(pos + 1, w).astype(F32)
        pooled.append((mean - u_new).astype(BF16))
    cols = slice(2 * pair * POOL_GROUP, 2 * (pair + 1) * POOL_GROUP)
    return (_dot(jnp.concatenate(pooled, axis=-1), pw_ref[pair]) + pb_ref[:, cols]) * ps_ref[:, cols]


def _fill_kv_variants(kx_ref, vx_ref, row0, kv):
    rows = kv.shape[0]
    low = lax.broadcasted_iota(jnp.int32, (rows, LANES), 1) < HEAD_DIM
    ones = (jnp.where(low, 1.0, 0.0).astype(BF16), jnp.where(low, 0.0, 1.0).astype(BF16))
    for t, base in ((0, 0), (1, D_KV)):
        both = kv[:, base:base + D_KV]
        swapped = pltpu.roll(both, HEAD_DIM, axis=1)
        variants = (jnp.where(low, both, 0.0), jnp.where(low, 0.0, swapped),
                    jnp.where(low, swapped, 0.0), jnp.where(low, 0.0, both))
        for i, val in enumerate(variants):
            if t == 0:
                kx_ref[i, row0:row0 + rows, :] = val.astype(BF16)
            else:
                vx_ref[i, row0:row0 + rows, :LANES] = val.astype(BF16)
                vx_ref[i, row0:row0 + rows, LANES:] = ones[i % 2]


def _sink_columns(sinks_ref, q_rows):
    m_rows = PAIRS_PER_KV * q_rows
    row_blk = lax.broadcasted_iota(jnp.int32, (m_rows, 1), 0) // q_rows
    cols = []
    for kh in range(N_KV_HEADS):
        pair = []
        for ab in range(2):
            sink = jnp.zeros((m_rows, 1), F32)
            for pp in range(PAIRS_PER_KV):
                sink = jnp.where(row_blk == pp, sinks_ref[kh * 2 * PAIRS_PER_KV + 2 * pp + ab] * LOG2E, sink)
            pair.append(sink)
        cols.append(pair)
    return cols


def _attend_block(q_ref, q_row0, q_rows, kx_ref, vx_ref, key_row0, n_keys, key_valid, sink_cols, yb_ref):
    m_rows = PAIRS_PER_KV * q_rows
    low = lax.broadcasted_iota(jnp.int32, (m_rows, LANES), 1) < HEAD_DIM
    for kh in range(N_KV_HEADS):
        qs = jnp.concatenate(
            [q_ref[q_row0:q_row0 + q_rows, (kh * PAIRS_PER_KV + pp) * LANES:(kh * PAIRS_PER_KV + pp + 1) * LANES]
             for pp in range(PAIRS_PER_KV)], axis=0)
        o = None
        row_max = []
        for ab in range(2):
            s = _dot_nt(qs, kx_ref[2 * kh + ab, key_row0:key_row0 + n_keys, :])
            if key_valid is not None:
                s = jnp.where(key_valid, s, NEG_BIG)
            m = jnp.maximum(jnp.max(s, axis=-1, keepdims=True), sink_cols[kh][ab])
            p = jnp.exp2(s - m).astype(BF16)
            pv = _dot(p, vx_ref[2 * kh + ab, key_row0:key_row0 + n_keys, :])
            o = pv if o is None else o + pv
            row_max.append(m)
        sink_p = jnp.exp2(jnp.where(low, sink_cols[kh][0], sink_cols[kh][1])
                          - jnp.where(low, row_max[0], row_max[1]))
        out = o[:, :LANES] / (o[:, LANES:] + sink_p)
        for pp in range(PAIRS_PER_KV):
            col = (kh * PAIRS_PER_KV + pp) * LANES
            yb_ref[q_row0:q_row0 + q_rows, col:col + LANES] = out[pp * q_rows:(pp + 1) * q_rows].astype(BF16)


def _gate_cols(h, wg_ref, bg_ref, lo, hi):
    return jax.nn.sigmoid(_dot(h, wg_ref[:, lo:hi]) + bg_ref[:, lo:hi])


def _pool_mixer(uext_ref, rows, first_row, pw_ref, pb_ref, ps_ref):
    return jnp.concatenate([_pool_pair(uext_ref, rows, first_row, pw_ref, pb_ref, ps_ref, i).astype(BF16)
                            for i in range(N_POOL_GROUPS // 2)], axis=-1)


def _mix_out(x1, gates, ya, yb, wpp_ref, wap_ref, wo_ref):
    a = _dot(ya, wpp_ref[...])
    b = _dot(yb, wap_ref[...])
    mixed = gates[:, :D_MODEL] * a + gates[:, D_MODEL:] * b
    return x1 + _dot(mixed.astype(BF16), wo_ref[...])


def _prompt_mix_body(sinks_ref, x1_ref, h_ref, u_ref, q_ref, kv_ref,
                     pw_ref, pb_ref, ps_ref, wg_ref, bg_ref, wpp_ref, wap_ref, wo_ref,
                     g2_ref, w1_ref, w2_ref,
                     o_ref, uext, kx, vx, yb_scr, a_scr, *, rows):
    t = pl.program_id(1)

    @pl.when(t == 0)
    def _():
        uext[:POOL_HIST_PAD, :] = jnp.zeros((POOL_HIST_PAD, D_POOL), F32)
        kx[:, :WINDOW, :] = jnp.zeros((4, WINDOW, LANES), BF16)
        vx[:, :WINDOW, :] = jnp.zeros((4, WINDOW, 2 * LANES), BF16)

    uext[POOL_HIST_PAD:, :] = u_ref[...]
    _fill_kv_variants(kx, vx, WINDOW, kv_ref[...])

    ya = _pool_mixer(uext, rows, t * rows, pw_ref, pb_ref, ps_ref)

    n_chunks = rows // CHUNK
    n_keys = WINDOW + CHUNK
    key_chunk = lax.broadcasted_iota(jnp.int32, (1, n_keys), 1) // CHUNK
    sink_cols = _sink_columns(sinks_ref, CHUNK)
    for j in range(n_chunks):
        first_chunk = t * n_chunks + j - WINDOW // CHUNK
        _attend_block(q_ref, j * CHUNK, CHUNK, kx, vx, j * CHUNK, n_keys, key_chunk + first_chunk >= 0,
                      sink_cols, yb_scr)

    gates = _gate_cols(h_ref[...], wg_ref, bg_ref, 0, 2 * D_MODEL)
    x2 = _mix_out(x1_ref[...], gates, ya, yb_scr[...], wpp_ref, wap_ref, wo_ref)
    o_ref[...] = _ffn(x2, g2_ref, w1_ref, w2_ref, a_scr)

    uext[:POOL_HIST_PAD, :] = uext[rows:rows + POOL_HIST_PAD, :]
    kx[:, :WINDOW, :] = kx[:, rows:rows + WINDOW, :]
    vx[:, :WINDOW, :] = vx[:, rows:rows + WINDOW, :]


def _sample_mix_body(sinks_ref, x1_ref, h_ref, u_ref, q_ref, kv_ref, hu_ref, hkv_ref,
                     pw_ref, pb_ref, ps_ref, wg_ref, bg_ref, wpp_ref, wap_ref, wo_ref,
                     o_ref, uext, kx, vx, yb_scr, *, rows):
    uext[:POOL_HIST_PAD, :] = hu_ref[...]
    uext[POOL_HIST_PAD:, :] = u_ref[...]
    _fill_kv_variants(kx, vx, 0, hkv_ref[...])
    _fill_kv_variants(kx, vx, WINDOW, kv_ref[...])
    ya = _pool_mixer(uext, rows, None, pw_ref, pb_ref, ps_ref)
    _attend_block(q_ref, 0, rows, kx, vx, 0, WINDOW + rows, None, _sink_columns(sinks_ref, rows), yb_scr)
    gates = _gate_cols(h_ref[...], wg_ref, bg_ref, 0, 2 * D_MODEL)
    o_ref[...] = _mix_out(x1_ref[...], gates, ya, yb_scr[...], wpp_ref, wap_ref, wo_ref)


def _const_spec(shape):
    return pl.BlockSpec(shape, lambda *_: (0,) * len(shape), pipeline_mode=pl.Buffered(1))


def _row_tile(n_rows, target=256):
    return target if n_rows % target == 0 else n_rows


def _front_call(x2d, g1, w1, w2, gm, win, qn2, kn2, cast_srcs=()):
    n = x2d.shape[0]
    tm = _row_tile(n, 512)
    steps = n // tm
    row = lambda width: pl.BlockSpec((tm, width), lambda i: (i, 0))
    cast_in, cast_out, cast_shapes = [], [], []
    for w, cols in cast_srcs:
        _, w_rows, w_cols = w.shape
        n_blk = max(d for d in range(1, steps + 1) if w_rows % (16 * d) == 0)
        rows = w_rows // n_blk
        assert cols % LANES == 0
        cast_in.append(pl.BlockSpec((None, rows, w_cols), lambda i, n_blk=n_blk: (0, jnp.minimum(i, n_blk - 1), 0)))
        cast_out.append(pl.BlockSpec((rows, cols), lambda i, n_blk=n_blk: (jnp.minimum(i, n_blk - 1), 0)))
        cast_shapes.append(jax.ShapeDtypeStruct((w_rows, cols), BF16))
    return pl.pallas_call(
        functools.partial(_front_body, n_cast=len(cast_srcs)),
        grid=(steps,),
        in_specs=[row(D_MODEL), _const_spec(g1.shape), _const_spec(w1.shape), _const_spec(w2.shape),
                  _const_spec(gm.shape), _const_spec(win.shape), _const_spec(qn2.shape), _const_spec(kn2.shape)]
                 + cast_in,
        out_specs=[row(D_MODEL), row(D_MODEL), row(D_POOL), row(D_Q), row(2 * D_KV)] + cast_out,
        out_shape=[jax.ShapeDtypeStruct((n, D_MODEL), F32), jax.ShapeDtypeStruct((n, D_MODEL), BF16),
                   jax.ShapeDtypeStruct((n, D_POOL), F32), jax.ShapeDtypeStruct((n, D_Q), BF16),
                   jax.ShapeDtypeStruct((n, 2 * D_KV), F32)] + cast_shapes,
        scratch_shapes=[pltpu.VMEM((tm, D_FF), BF16)],
        compiler_params=pltpu.CompilerParams(dimension_semantics=("arbitrary",),
                                             vmem_limit_bytes=VMEM_LIMIT_BYTES),
        name="front",
    )(x2d, g1, w1, w2, gm, win, qn2, kn2, *[w for w, _ in cast_srcs])


def _ffn_call(x2d, g, w1, w2):
    n = x2d.shape[0]
    tm = _row_tile(n)
    row = pl.BlockSpec((tm, D_MODEL), lambda i: (i, 0))
    return pl.pallas_call(
        _ffn_body,
        grid=(n // tm,),
        in_specs=[row, _const_spec(g.shape), _const_spec(w1.shape), _const_spec(w2.shape)],
        out_specs=row,
        out_shape=jax.ShapeDtypeStruct((n, D_MODEL), F32),
        scratch_shapes=[pltpu.VMEM((tm, D_FF), BF16)],
        compiler_params=pltpu.CompilerParams(dimension_semantics=("parallel",),
                                             vmem_limit_bytes=VMEM_LIMIT_BYTES),
        name="ffn",
    )(x2d, g, w1, w2)


def _prompt_mix_call(sinks, x1, h, u, q, kv, mix_w, ffn_w):
    b, s, _ = x1.shape
    tm = _row_tile(s, 512)
    row = lambda width: pl.BlockSpec((None, tm, width), lambda i, t: (i, t, 0))
    weights = tuple(mix_w) + tuple(ffn_w)
    return pl.pallas_call(
        functools.partial(_prompt_mix_body, rows=tm),
        grid=(b, s // tm),
        in_specs=[pl.BlockSpec(memory_space=pltpu.SMEM),
                  row(D_MODEL), row(D_MODEL), row(D_POOL), row(D_Q), row(2 * D_KV)]
                 + [_const_spec(w.shape) for w in weights],
        out_specs=row(D_MODEL),
        out_shape=jax.ShapeDtypeStruct((b, s, D_MODEL), F32),
        scratch_shapes=[pltpu.VMEM((POOL_HIST_PAD + tm, D_POOL), F32),
                        pltpu.VMEM((4, WINDOW + tm, LANES), BF16),
                        pltpu.VMEM((4, WINDOW + tm, 2 * LANES), BF16),
                        pltpu.VMEM((tm, D_Q), BF16),
                        pltpu.VMEM((tm, D_FF), BF16)],
        compiler_params=pltpu.CompilerParams(dimension_semantics=("arbitrary", "arbitrary"),
                                             vmem_limit_bytes=VMEM_LIMIT_BYTES),
        name="prompt_mix",
    )(sinks, x1, h, u, q, kv, *weights)


def _sample_mix_call(sinks, x1, h, u, q, kv, hist_u, hist_kv, mix_w):
    b, t, _ = x1.shape
    row = lambda rows, width: pl.BlockSpec((None, rows, width), lambda i: (i, 0, 0))
    return pl.pallas_call(
        functools.partial(_sample_mix_body, rows=t),
        grid=(b,),
        in_specs=[pl.BlockSpec(memory_space=pltpu.SMEM),
                  row(t, D_MODEL), row(t, D_MODEL), row(t, D_POOL), row(t, D_Q), row(t, 2 * D_KV),
                  row(POOL_HIST_PAD, D_POOL), row(WINDOW, 2 * D_KV)]
                 + [_const_spec(w.shape) for w in mix_w],
        out_specs=row(t, D_MODEL),
        out_shape=jax.ShapeDtypeStruct((b, t, D_MODEL), F32),
        scratch_shapes=[pltpu.VMEM((POOL_HIST_PAD + t, D_POOL), F32),
                        pltpu.VMEM((4, WINDOW + t, LANES), BF16),
                        pltpu.VMEM((4, WINDOW + t, 2 * LANES), BF16),
                        pltpu.VMEM((t, D_Q), BF16)],
        compiler_params=pltpu.CompilerParams(dimension_semantics=("arbitrary",),
                                             vmem_limit_bytes=VMEM_LIMIT_BYTES),
        name="sample_mix",
    )(sinks, x1, h, u, q, kv, hist_u, hist_kv, *mix_w)


def _ffn_weights(norm, w_in, w_out):
    return norm.reshape(1, D_MODEL), w_in.astype(BF16), w_out.astype(BF16)


def kernel(x_prompt, x_sample, state_pool, cache_k, cache_v, norm_ffn1, ffn1_w_in, ffn1_w_out, norm_mix, w_in, b_gate, pool_w, pool_b, pool_scale, q_norm, k_norm, sinks, w_pool_proj, w_attn_proj, w_out, norm_ffn2, ffn2_w_in, ffn2_w_out):
    depth = norm_ffn1.shape[0]
    assert depth == 1, "single-layer trunk"
    bp, sp, _ = x_prompt.shape
    bs, ts, _ = x_sample.shape

    ffn1 = _ffn_weights(norm_ffn1[0], ffn1_w_in[0], ffn1_w_out[0])
    front_w = ffn1 + (norm_mix[0].reshape(1, D_MODEL), w_in[0, :, :D_UQKV].astype(BF16),
                      jnp.tile(q_norm[0], 2).reshape(1, LANES), jnp.tile(k_norm[0], 2).reshape(1, LANES))
    cast_srcs = ((w_in, 2 * D_MODEL), (w_pool_proj, D_MODEL), (w_attn_proj, D_MODEL), (w_out, D_MODEL),
                 (ffn2_w_in, 2 * D_FF), (ffn2_w_out, D_MODEL))
    sink_vec = sinks[0]

    x1, h, u, q, kv, wg, wpp, wap, wo, ffn2_w1, ffn2_w2 = _front_call(
        x_prompt.reshape(bp * sp, D_MODEL), *front_w, cast_srcs=cast_srcs)
    ffn2 = (norm_ffn2[0].reshape(1, D_MODEL), ffn2_w1, ffn2_w2)
    pw = pool_w[0].astype(BF16)
    zero_blk = jnp.zeros((POOL_GROUP, POOL_GROUP), BF16)
    pw_pairs = jnp.stack([jnp.block([[pw[2 * i], zero_blk], [zero_blk, pw[2 * i + 1]]])
                          for i in range(N_POOL_GROUPS // 2)])
    mix_w = (pw_pairs, pool_b[0].reshape(1, D_POOL), pool_scale[0].reshape(1, D_POOL),
             wg, b_gate[0].reshape(1, 2 * D_MODEL), wpp, wap, wo)
    shp = lambda a: a.reshape(bp, sp, a.shape[-1])
    y_prompt = _prompt_mix_call(sink_vec, shp(x1), shp(h), shp(u), shp(q), shp(kv), mix_w, ffn2)
    u3, kv3 = shp(u), shp(kv)
    new_pool_p = u3[:, sp - POOL_HIST:]
    new_k_p = kv3[:, sp - WINDOW:, :D_KV].reshape(bp, WINDOW, N_KV_HEADS, HEAD_DIM)
    new_v_p = kv3[:, sp - WINDOW:, D_KV:].reshape(bp, WINDOW, N_KV_HEADS, HEAD_DIM)

    x1s, hs, us, qs, kvs = _front_call(x_sample.reshape(bs * ts, D_MODEL), *front_w)
    shs = lambda a: a.reshape(bs, ts, a.shape[-1])
    hist_u = jnp.pad(state_pool[0], ((0, 0), (POOL_HIST_PAD - POOL_HIST, 0), (0, 0)))
    hist_kv = jnp.concatenate([cache_k[0].reshape(bs, WINDOW, D_KV), cache_v[0].reshape(bs, WINDOW, D_KV)], axis=-1)
    x2s = _sample_mix_call(sink_vec, shs(x1s), shs(hs), shs(us), shs(qs), shs(kvs), hist_u, hist_kv, mix_w)
    y_sample = _ffn_call(x2s.reshape(bs * ts, D_MODEL), *ffn2).reshape(bs, ts, D_MODEL)
    us3, kvs3 = shs(us), shs(kvs)
    new_pool_s = jnp.concatenate([state_pool[0], us3], axis=1)[:, -POOL_HIST:]
    kv_full = jnp.concatenate([hist_kv, kvs3], axis=1)[:, -WINDOW:]
    new_k_s = kv_full[..., :D_KV].reshape(bs, WINDOW, N_KV_HEADS, HEAD_DIM)
    new_v_s = kv_full[..., D_KV:].reshape(bs, WINDOW, N_KV_HEADS, HEAD_DIM)

    return (y_prompt, y_sample, new_pool_p[None], new_k_p[None], new_v_p[None],
            new_pool_s[None], new_k_s[None], new_v_s[None])
```

```python
import functools

import jax
import jax.numpy as jnp
from jax import lax
from jax.experimental import pallas as pl
from jax.experimental.pallas import tpu as pltpu

D_MODEL = 1024
CHUNK = 64
N_HEADS = 16
N_KV_HEADS = 2
HEAD_DIM = 64
WINDOW = 128
D_POOL = 512
N_POOL_GROUPS = 4
POOL_GROUP = 128
POOL_WINDOWS = (2, 4, 8, 16)
POOL_HIST = 15
POOL_HIST_PAD = 16
D_Q = N_HEADS * HEAD_DIM
D_KV = N_KV_HEADS * HEAD_DIM
D_UQKV = D_POOL + D_Q + 2 * D_KV
D_FF = 2816
EPS = 1e-6

LANES = 128
BF16_SUBLANES = 16
FF_CHUNK = 256
N_FF_CHUNKS = D_FF // FF_CHUNK
PAIRS_PER_KV = (N_HEADS // N_KV_HEADS) // 2
ROW_TILE = 512
VMEM_LIMIT_BYTES = 56 * 1024 * 1024
NEG_BIG = float(jnp.finfo(jnp.float32).min)
LOG2E = 1.4426950408889634

BF16 = jnp.bfloat16
F32 = jnp.float32


def _dot(a, b):
    return jnp.dot(a, b, preferred_element_type=F32)


def _dot_nt(a, b):
    return lax.dot_general(a, b, (((1,), (1,)), ((), ())), preferred_element_type=F32)


def _rmsnorm(x, g):
    return x * lax.rsqrt(jnp.mean(x * x, axis=-1, keepdims=True) + EPS) * g


def _ffn(x, g_ref, w1_ref, w2_ref, a_scr, side_work=()):
    h = _rmsnorm(x, g_ref[...]).astype(BF16)
    for c in range(N_FF_CHUNKS):
        gate = _dot(h, w1_ref[:, c * FF_CHUNK:(c + 1) * FF_CHUNK])
        up = _dot(h, w1_ref[:, D_FF + c * FF_CHUNK:D_FF + (c + 1) * FF_CHUNK])
        a_scr[:, c * FF_CHUNK:(c + 1) * FF_CHUNK] = (gate * jax.nn.sigmoid(gate) * up).astype(BF16)
        if c < len(side_work):
            side_work[c]()
    assert len(side_work) <= N_FF_CHUNKS
    return x + 0.5 * _dot(a_scr[...], w2_ref[...])


def _half_sumsq(x):
    y = x * x
    low = lax.broadcasted_iota(jnp.int32, y.shape, 1) < HEAD_DIM
    sum_low = jnp.sum(jnp.where(low, y, 0.0), axis=-1, keepdims=True)
    sum_high = jnp.sum(jnp.where(low, 0.0, y), axis=-1, keepdims=True)
    return jnp.where(low, sum_low, sum_high)


def _head_norm(x, g2):
    return x * lax.rsqrt(_half_sumsq(x) * (1.0 / HEAD_DIM) + EPS) * g2


def _cast_block(src_ref, dst_ref):
    dst_ref[...] = src_ref[:, src_ref.shape[1] - dst_ref.shape[1]:].astype(BF16)


def _front_body(*refs, n_cast, n_prompt_steps):
    xp_ref, xs_ref, g1_ref, w1_ref, w2_ref, gm_ref, win_ref, qn_ref, kn_ref = refs[:9]
    cast_src = refs[9:9 + n_cast]
    x1_ref, h_ref, u_ref, q_ref, kv_ref = refs[9 + n_cast:14 + n_cast]
    cast_dst = refs[14 + n_cast:14 + 2 * n_cast]
    a_scr = refs[14 + 2 * n_cast]
    casts = [functools.partial(_cast_block, src, dst) for src, dst in zip(cast_src, cast_dst)]
    x = jnp.where(pl.program_id(0) < n_prompt_steps, xp_ref[...], xs_ref[...])
    x1 = _ffn(x, g1_ref, w1_ref, w2_ref, a_scr, casts)
    x1_ref[...] = x1
    h = _rmsnorm(x1, gm_ref[...]).astype(BF16)
    h_ref[...] = h
    qg = qn_ref[...] * (HEAD_DIM ** -0.5 * LOG2E)
    blk = 2 * LANES
    for b in range(D_Q // blk):
        zq = _dot(h, win_ref[:, D_POOL + b * blk:D_POOL + (b + 1) * blk])
        for p in range(2):
            q_ref[:, b * blk + p * LANES:b * blk + (p + 1) * LANES] = _head_norm(
                zq[:, p * LANES:(p + 1) * LANES], qg).astype(BF16)
    zkv = _dot(h, win_ref[:, D_POOL + D_Q:D_POOL + D_Q + 2 * D_KV])
    kv_ref[:, :D_KV] = _head_norm(zkv[:, :D_KV], kn_ref[...])
    kv_ref[:, D_KV:] = zkv[:, D_KV:]
    u_ref[...] = _dot(h, win_ref[:, :D_POOL])


def _pool_pair(ext_ref, take_new, first_row, pw_ref, pb_ref, ps_ref, pair):
    pooled = []
    for gi in (2 * pair, 2 * pair + 1):
        w = POOL_WINDOWS[gi]
        ext = ext_ref[:, gi * POOL_GROUP:(gi + 1) * POOL_GROUP]
        u_new = take_new(ext)
        s = ext
        k = 1
        while k < w:
            s = s + pltpu.roll(s, k, axis=0)
            k *= 2
        s = take_new(s)
        if first_row is None:
            mean = s * (1.0 / w)
        else:
            pos = first_row + lax.broadcasted_iota(jnp.int32, (s.shape[0], 1), 0)
            mean = s / jnp.minimum(pos + 1, w).astype(F32)
        pooled.append((mean - u_new).astype(BF16))
    cols = slice(2 * pair * POOL_GROUP, 2 * (pair + 1) * POOL_GROUP)
    return (_dot(jnp.concatenate(pooled, axis=-1), pw_ref[pair]) + pb_ref[:, cols]) * ps_ref[:, cols]


def _pool_mixer(ext_ref, take_new, first_row, pw_ref, pb_ref, ps_ref):
    return jnp.concatenate([_pool_pair(ext_ref, take_new, first_row, pw_ref, pb_ref, ps_ref, i).astype(BF16)
                            for i in range(N_POOL_GROUPS // 2)], axis=-1)


def _fill_kv_variants(kx_ref, vx_ref, row0, kv):
    rows = kv.shape[0]
    low = lax.broadcasted_iota(jnp.int32, (rows, LANES), 1) < HEAD_DIM
    ones = (jnp.where(low, 1.0, 0.0).astype(BF16), jnp.where(low, 0.0, 1.0).astype(BF16))
    for t, base in ((0, 0), (1, D_KV)):
        both = kv[:, base:base + D_KV]
        swapped = pltpu.roll(both, HEAD_DIM, axis=1)
        variants = (jnp.where(low, both, 0.0), jnp.where(low, 0.0, swapped),
                    jnp.where(low, swapped, 0.0), jnp.where(low, 0.0, both))
        for i, val in enumerate(variants):
            if t == 0:
                kx_ref[i, row0:row0 + rows, :] = val.astype(BF16)
            else:
                vx_ref[i, row0:row0 + rows, :LANES] = val.astype(BF16)
                vx_ref[i, row0:row0 + rows, LANES:] = ones[i % 2]


def _sink_columns(sinks_ref, q_rows):
    m_rows = PAIRS_PER_KV * q_rows
    row_blk = lax.broadcasted_iota(jnp.int32, (m_rows, 1), 0) // q_rows
    cols = []
    for kh in range(N_KV_HEADS):
        pair = []
        for ab in range(2):
            sink = jnp.zeros((m_rows, 1), F32)
            for pp in range(PAIRS_PER_KV):
                sink = jnp.where(row_blk == pp, sinks_ref[kh * 2 * PAIRS_PER_KV + 2 * pp + ab] * LOG2E, sink)
            pair.append(sink)
        cols.append(pair)
    return cols


def _attend_block(q_ref, q_row0, q_rows, kx_ref, vx_ref, key_row0, n_keys, key_valid, sink_cols, yb_ref):
    m_rows = PAIRS_PER_KV * q_rows
    low = lax.broadcasted_iota(jnp.int32, (m_rows, LANES), 1) < HEAD_DIM
    for kh in range(N_KV_HEADS):
        qs = jnp.concatenate(
            [q_ref[q_row0:q_row0 + q_rows, (kh * PAIRS_PER_KV + pp) * LANES:(kh * PAIRS_PER_KV + pp + 1) * LANES]
             for pp in range(PAIRS_PER_KV)], axis=0)
        o = None
        row_max = []
        for ab in range(2):
            s = _dot_nt(qs, kx_ref[2 * kh + ab, key_row0:key_row0 + n_keys, :])
            if key_valid is not None:
                s = jnp.where(key_valid, s, NEG_BIG)
            m = jnp.maximum(jnp.max(s, axis=-1, keepdims=True), sink_cols[kh][ab])
            p = jnp.exp2(s - m).astype(BF16)
            pv = _dot(p, vx_ref[2 * kh + ab, key_row0:key_row0 + n_keys, :])
            o = pv if o is None else o + pv
            row_max.append(m)
        sink_p = jnp.exp2(jnp.where(low, sink_cols[kh][0], sink_cols[kh][1])
                          - jnp.where(low, row_max[0], row_max[1]))
        out = o[:, :LANES] / (o[:, LANES:] + sink_p)
        for pp in range(PAIRS_PER_KV):
            col = (kh * PAIRS_PER_KV + pp) * LANES
            yb_ref[q_row0:q_row0 + q_rows, col:col + LANES] = out[pp * q_rows:(pp + 1) * q_rows].astype(BF16)


def _dense_tail(x1, h, ya, yb, wg_ref, bg_ref, wpp_ref, wap_ref, wo_ref, g2_ref, w1_ref, w2_ref, a_scr):
    gates = jax.nn.sigmoid(_dot(h, wg_ref[...]) + bg_ref[...])
    a = _dot(ya, wpp_ref[...])
    b = _dot(yb, wap_ref[...])
    mixed = gates[:, :D_MODEL] * a + gates[:, D_MODEL:] * b
    x2 = x1 + _dot(mixed.astype(BF16), wo_ref[...])
    return _ffn(x2, g2_ref, w1_ref, w2_ref, a_scr)


def _prompt_mix_body(sinks_ref, x1_ref, h_ref, u_ref, q_ref, kv_ref,
                     pw_ref, pb_ref, ps_ref, wg_ref, bg_ref, wpp_ref, wap_ref, wo_ref,
                     g2_ref, w1_ref, w2_ref,
                     o_ref, uext, kx, vx, yb_scr, a_scr, *, rows):
    t = pl.program_id(1)

    @pl.when(t == 0)
    def _():
        uext[:POOL_HIST_PAD, :] = jnp.zeros((POOL_HIST_PAD, D_POOL), F32)
        kx[:, :WINDOW, :] = jnp.zeros((4, WINDOW, LANES), BF16)
        vx[:, :WINDOW, :] = jnp.zeros((4, WINDOW, 2 * LANES), BF16)

    uext[POOL_HIST_PAD:, :] = u_ref[...]
    _fill_kv_variants(kx, vx, WINDOW, kv_ref[...])

    ya = _pool_mixer(uext, lambda a: a[POOL_HIST_PAD:], t * rows, pw_ref, pb_ref, ps_ref)

    n_chunks = rows // CHUNK
    n_keys = WINDOW + CHUNK
    key_chunk = lax.broadcasted_iota(jnp.int32, (1, n_keys), 1) // CHUNK
    sink_cols = _sink_columns(sinks_ref, CHUNK)
    for j in range(n_chunks):
        first_chunk = t * n_chunks + j - WINDOW // CHUNK
        _attend_block(q_ref, j * CHUNK, CHUNK, kx, vx, j * CHUNK, n_keys, key_chunk + first_chunk >= 0,
                      sink_cols, yb_scr)

    o_ref[...] = _dense_tail(x1_ref[...], h_ref[...], ya, yb_scr[...], wg_ref, bg_ref, wpp_ref, wap_ref, wo_ref,
                             g2_ref, w1_ref, w2_ref, a_scr)

    uext[:POOL_HIST_PAD, :] = uext[rows:rows + POOL_HIST_PAD, :]
    kx[:, :WINDOW, :] = kx[:, rows:rows + WINDOW, :]
    vx[:, :WINDOW, :] = vx[:, rows:rows + WINDOW, :]


def _sample_tail_body(sinks_ref, x1_ref, h_ref, u_ref, q_ref, kv_ref, hu_ref, hkv_ref,
                      pw_ref, pb_ref, ps_ref, wg_ref, bg_ref, wpp_ref, wap_ref, wo_ref,
                      g2_ref, w1_ref, w2_ref,
                      o_ref, uext, kx, vx, yb_scr, a_scr, *, n_streams, t_new):
    seg = POOL_HIST_PAD + t_new
    for e in range(n_streams):
        uext[e * seg:e * seg + POOL_HIST_PAD, :] = hu_ref[e]
        uext[e * seg + POOL_HIST_PAD:(e + 1) * seg, :] = u_ref[e * t_new:(e + 1) * t_new, :]

    def take_new(a):
        return jnp.concatenate([a[e * seg + POOL_HIST_PAD:(e + 1) * seg] for e in range(n_streams)], axis=0)

    ya = _pool_mixer(uext, take_new, None, pw_ref, pb_ref, ps_ref)

    sink_cols = _sink_columns(sinks_ref, t_new)
    for e in range(n_streams):
        kx_e, vx_e = kx.at[e % 2], vx.at[e % 2]
        _fill_kv_variants(kx_e, vx_e, 0, hkv_ref[e])
        _fill_kv_variants(kx_e, vx_e, WINDOW, kv_ref[e * t_new:(e + 1) * t_new, :])
        _attend_block(q_ref, e * t_new, t_new, kx_e, vx_e, 0, WINDOW + t_new, None, sink_cols, yb_scr)

    o_ref[...] = _dense_tail(x1_ref[...], h_ref[...], ya, yb_scr[...], wg_ref, bg_ref, wpp_ref, wap_ref, wo_ref,
                             g2_ref, w1_ref, w2_ref, a_scr)


def _const_spec(shape):
    return pl.BlockSpec(shape, lambda *_: (0,) * len(shape), pipeline_mode=pl.Buffered(1))


def _front_call(xp2d, xs2d, g1, w1, w2, gm, win, qn2, kn2, cast_srcs):
    n_p = xp2d.shape[0]
    tm = ROW_TILE
    assert n_p % tm == 0 and xs2d.shape[0] == tm
    n_prompt_steps = n_p // tm
    steps = n_prompt_steps + 1
    n = n_p + tm
    row = lambda width: pl.BlockSpec((tm, width), lambda i: (i, 0))
    cast_in, cast_out, cast_shapes = [], [], []
    for w, cols in cast_srcs:
        _, w_rows, w_cols = w.shape
        n_blk = max(d for d in range(1, steps + 1) if w_rows % (BF16_SUBLANES * d) == 0)
        rows = w_rows // n_blk
        assert cols % LANES == 0
        cast_in.append(pl.BlockSpec((None, rows, w_cols), lambda i, n_blk=n_blk: (0, jnp.minimum(i, n_blk - 1), 0)))
        cast_out.append(pl.BlockSpec((rows, cols), lambda i, n_blk=n_blk: (jnp.minimum(i, n_blk - 1), 0)))
        cast_shapes.append(jax.ShapeDtypeStruct((w_rows, cols), BF16))
    return pl.pallas_call(
        functools.partial(_front_body, n_cast=len(cast_srcs), n_prompt_steps=n_prompt_steps),
        grid=(steps,),
        in_specs=[pl.BlockSpec((tm, D_MODEL), lambda i: (jnp.minimum(i, n_prompt_steps - 1), 0)),
                  _const_spec(xs2d.shape),
                  _const_spec(g1.shape), _const_spec(w1.shape), _const_spec(w2.shape),
                  _const_spec(gm.shape), _const_spec(win.shape), _const_spec(qn2.shape), _const_spec(kn2.shape)]
                 + cast_in,
        out_specs=[row(D_MODEL), row(D_MODEL), row(D_POOL), row(D_Q), row(2 * D_KV)] + cast_out,
        out_shape=[jax.ShapeDtypeStruct((n, D_MODEL), F32), jax.ShapeDtypeStruct((n, D_MODEL), BF16),
                   jax.ShapeDtypeStruct((n, D_POOL), F32), jax.ShapeDtypeStruct((n, D_Q), BF16),
                   jax.ShapeDtypeStruct((n, 2 * D_KV), F32)] + cast_shapes,
        scratch_shapes=[pltpu.VMEM((tm, D_FF), BF16)],
        compiler_params=pltpu.CompilerParams(dimension_semantics=("arbitrary",),
                                             vmem_limit_bytes=VMEM_LIMIT_BYTES),
        name="front",
    )(xp2d, xs2d, g1, w1, w2, gm, win, qn2, kn2, *[w for w, _ in cast_srcs])


def _prompt_mix_call(sinks, x1, h, u, q, kv, mix_w, ffn_w, n_seq, seq_len):
    tm = ROW_TILE
    assert seq_len % tm == 0
    tiles = seq_len // tm
    row = lambda width: pl.BlockSpec((tm, width), lambda b, t: (b * tiles + t, 0))
    weights = tuple(mix_w) + tuple(ffn_w)
    return pl.pallas_call(
        functools.partial(_prompt_mix_body, rows=tm),
        grid=(n_seq, tiles),
        in_specs=[pl.BlockSpec(memory_space=pltpu.SMEM),
                  row(D_MODEL), row(D_MODEL), row(D_POOL), row(D_Q), row(2 * D_KV)]
                 + [_const_spec(w.shape) for w in weights],
        out_specs=row(D_MODEL),
        out_shape=jax.ShapeDtypeStruct((n_seq * seq_len, D_MODEL), F32),
        scratch_shapes=[pltpu.VMEM((POOL_HIST_PAD + tm, D_POOL), F32),
                        pltpu.VMEM((4, WINDOW + tm, LANES), BF16),
                        pltpu.VMEM((4, WINDOW + tm, 2 * LANES), BF16),
                        pltpu.VMEM((tm, D_Q), BF16),
                        pltpu.VMEM((tm, D_FF), BF16)],
        compiler_params=pltpu.CompilerParams(dimension_semantics=("arbitrary", "arbitrary"),
                                             vmem_limit_bytes=VMEM_LIMIT_BYTES),
        name="prompt_mix",
    )(sinks, x1, h, u, q, kv, *weights)


def _sample_tail_call(sinks, x1, h, u, q, kv, hist_u, hist_kv, mix_w, ffn_w, row_block, n_streams, t_new):
    tm = ROW_TILE
    assert n_streams * t_new == tm and t_new % BF16_SUBLANES == 0
    row = lambda width: pl.BlockSpec((tm, width), lambda i: (row_block, 0))
    weights = tuple(mix_w) + tuple(ffn_w)
    return pl.pallas_call(
        functools.partial(_sample_tail_body, n_streams=n_streams, t_new=t_new),
        grid=(1,),
        in_specs=[pl.BlockSpec(memory_space=pltpu.SMEM),
                  row(D_MODEL), row(D_MODEL), row(D_POOL), row(D_Q), row(2 * D_KV),
                  _const_spec(hist_u.shape), _const_spec(hist_kv.shape)]
                 + [_const_spec(w.shape) for w in weights],
        out_specs=pl.BlockSpec((tm, D_MODEL), lambda i: (0, 0)),
        out_shape=jax.ShapeDtypeStruct((tm, D_MODEL), F32),
        scratch_shapes=[pltpu.VMEM((n_streams * (POOL_HIST_PAD + t_new), D_POOL), F32),
                        pltpu.VMEM((2, 4, WINDOW + t_new, LANES), BF16),
                        pltpu.VMEM((2, 4, WINDOW + t_new, 2 * LANES), BF16),
                        pltpu.VMEM((tm, D_Q), BF16),
                        pltpu.VMEM((tm, D_FF), BF16)],
        compiler_params=pltpu.CompilerParams(dimension_semantics=("arbitrary",),
                                             vmem_limit_bytes=VMEM_LIMIT_BYTES),
        name="sample_tail",
    )(sinks, x1, h, u, q, kv, hist_u, hist_kv, *weights)


def kernel(x_prompt, x_sample, state_pool, cache_k, cache_v, norm_ffn1, ffn1_w_in, ffn1_w_out, norm_mix, w_in, b_gate, pool_w, pool_b, pool_scale, q_norm, k_norm, sinks, w_pool_proj, w_attn_proj, w_out, norm_ffn2, ffn2_w_in, ffn2_w_out):
    depth = norm_ffn1.shape[0]
    assert depth == 1, "single-layer trunk"
    bp, sp, _ = x_prompt.shape
    bs, ts, _ = x_sample.shape
    n_p = bp * sp

    front_w = (norm_ffn1[0].reshape(1, D_MODEL), ffn1_w_in[0].astype(BF16), ffn1_w_out[0].astype(BF16),
               norm_mix[0].reshape(1, D_MODEL), w_in[0, :, :D_UQKV].astype(BF16),
               jnp.tile(q_norm[0], 2).reshape(1, LANES), jnp.tile(k_norm[0], 2).reshape(1, LANES))
    cast_srcs = ((w_in, 2 * D_MODEL), (w_pool_proj, D_MODEL), (w_attn_proj, D_MODEL), (w_out, D_MODEL),
                 (ffn2_w_in, 2 * D_FF), (ffn2_w_out, D_MODEL))
    x1, h, u, q, kv, wg, wpp, wap, wo, ffn2_w1, ffn2_w2 = _front_call(
        x_prompt.reshape(n_p, D_MODEL), x_sample.reshape(bs * ts, D_MODEL), *front_w, cast_srcs=cast_srcs)

    ffn2 = (norm_ffn2[0].reshape(1, D_MODEL), ffn2_w1, ffn2_w2)
    pw = pool_w[0].astype(BF16)
    zero_blk = jnp.zeros((POOL_GROUP, POOL_GROUP), BF16)
    pw_pairs = jnp.stack([jnp.block([[pw[2 * i], zero_blk], [zero_blk, pw[2 * i + 1]]])
                          for i in range(N_POOL_GROUPS // 2)])
    mix_w = (pw_pairs, pool_b[0].reshape(1, D_POOL), pool_scale[0].reshape(1, D_POOL),
             wg, b_gate[0].reshape(1, 2 * D_MODEL), wpp, wap, wo)
    sink_vec = sinks[0]

    y_prompt = _prompt_mix_call(sink_vec, x1, h, u, q, kv, mix_w, ffn2, bp, sp).reshape(bp, sp, D_MODEL)
    new_pool_p = jnp.stack([u[(b + 1) * sp - POOL_HIST:(b + 1) * sp] for b in range(bp)])
    kv_last = jnp.stack([kv[(b + 1) * sp - WINDOW:(b + 1) * sp] for b in range(bp)])
    new_k_p = kv_last[..., :D_KV].reshape(bp, WINDOW, N_KV_HEADS, HEAD_DIM)
    new_v_p = kv_last[..., D_KV:].reshape(bp, WINDOW, N_KV_HEADS, HEAD_DIM)

    hist_u = jnp.pad(state_pool[0], ((0, 0), (POOL_HIST_PAD - POOL_HIST, 0), (0, 0)))
    hist_kv = jnp.concatenate([cache_k[0].reshape(bs, WINDOW, D_KV), cache_v[0].reshape(bs, WINDOW, D_KV)], axis=-1)
    y_sample = _sample_tail_call(sink_vec, x1, h, u, q, kv, hist_u, hist_kv, mix_w, ffn2,
                                 n_p // ROW_TILE, bs, ts).reshape(bs, ts, D_MODEL)
    us3 = u[n_p:].reshape(bs, ts, D_POOL)
    kvs3 = kv[n_p:].reshape(bs, ts, 2 * D_KV)
    new_pool_s = jnp.concatenate([state_pool[0], us3], axis=1)[:, -POOL_HIST:]
    kv_full = jnp.concatenate([hist_kv, kvs3], axis=1)[:, -WINDOW:]
    new_k_s = kv_full[..., :D_KV].reshape(bs, WINDOW, N_KV_HEADS, HEAD_DIM)
    new_v_s = kv_full[..., D_KV:].reshape(bs, WINDOW, N_KV_HEADS, HEAD_DIM)

    return (y_prompt, y_sample, new_pool_p[None], new_k_p[None], new_v_p[None],
            new_pool_s[None], new_k_s[None], new_v_s[None])
```

```python
import functools

import jax
import jax.numpy as jnp
from jax import lax
from jax.experimental import pallas as pl
from jax.experimental.pallas import tpu as pltpu

D_MODEL = 1024
CHUNK = 64
N_HEADS = 16
N_KV_HEADS = 2
HEAD_DIM = 64
WINDOW = 128
D_POOL = 512
N_POOL_GROUPS = 4
POOL_GROUP = 128
POOL_WINDOWS = (2, 4, 8, 16)
POOL_HIST = 15
POOL_HIST_PAD = 16
D_Q = N_HEADS * HEAD_DIM
D_KV = N_KV_HEADS * HEAD_DIM
D_UQKV = D_POOL + D_Q + 2 * D_KV
D_FF = 2816
EPS = 1e-6

LANES = 128
BF16_SUBLANES = 16
FF_CHUNK = 256
N_FF_CHUNKS = D_FF // FF_CHUNK
PAIRS_PER_KV = (N_HEADS // N_KV_HEADS) // 2
ROW_TILE = 512
PROJ_BLOCK_ORDER = (2, 3, 4, 5, 6, 0, 1)
WEIGHT_STAGE_SLOTS = (3, 2)
VMEM_LIMIT_BYTES = 56 * 1024 * 1024
NEG_BIG = float(jnp.finfo(jnp.float32).min)
LOG2E = 1.4426950408889634

BF16 = jnp.bfloat16
F32 = jnp.float32


def _dot(a, b):
    return jnp.dot(a, b, preferred_element_type=F32)


def _dot_nt(a, b):
    return lax.dot_general(a, b, (((1,), (1,)), ((), ())), preferred_element_type=F32)


def _rms_scale(x):
    return lax.rsqrt(jnp.mean(x * x, axis=-1, keepdims=True) + EPS)


def _ffn(x, g_ref, w1_ref, w2_ref, a_scr, side_work=(), loader=None):
    xg = (x * g_ref[...]).astype(BF16)
    r = _rms_scale(x)
    for c in range(N_FF_CHUNKS):
        if loader is not None:
            loader.before_ffn_chunk(c)
        gate = _dot(xg, w1_ref[:, c * FF_CHUNK:(c + 1) * FF_CHUNK]) * r
        up = _dot(xg, w1_ref[:, D_FF + c * FF_CHUNK:D_FF + (c + 1) * FF_CHUNK]) * r
        a_scr[:, c * FF_CHUNK:(c + 1) * FF_CHUNK] = (gate * jax.nn.sigmoid(gate) * up).astype(BF16)
        if c < len(side_work):
            side_work[c]()
    assert len(side_work) <= N_FF_CHUNKS
    if loader is not None:
        loader.before_ffn_out()
    return x + 0.5 * _dot(a_scr[...], w2_ref[...])


def _half_sumsq(x):
    y = x * x
    low = lax.broadcasted_iota(jnp.int32, y.shape, 1) < HEAD_DIM
    sum_low = jnp.sum(jnp.where(low, y, 0.0), axis=-1, keepdims=True)
    sum_high = jnp.sum(jnp.where(low, 0.0, y), axis=-1, keepdims=True)
    return jnp.where(low, sum_low, sum_high)


def _head_norm(x, g2):
    return x * lax.rsqrt(_half_sumsq(x) * (1.0 / HEAD_DIM) + EPS) * g2


def _cast_block(src_ref, dst_ref):
    dst_ref[...] = src_ref[:, src_ref.shape[1] - dst_ref.shape[1]:].astype(BF16)


class _WeightLoader:
    def __init__(self, w1_hbm, w2_hbm, win_hbm, w1_bf, w2_bf, win_bf, stage_cols, stage_rows, sem_cols, sem_rows):
        blk = FF_CHUNK
        self._units = {"cols": [], "rows": []}
        self._stage = {"cols": stage_cols, "rows": stage_rows}
        self._sem = {"cols": sem_cols, "rows": sem_rows}
        self._done = {"cols": 0, "rows": 0}

        def col_unit(hbm, dst, col):
            return (hbm.at[0, :, pl.ds(col, blk)], dst, (slice(None), slice(col, col + blk)))

        for c in range(N_FF_CHUNKS):
            self._units["cols"].append(col_unit(w1_hbm, w1_bf, c * blk))
            self._units["cols"].append(col_unit(w1_hbm, w1_bf, D_FF + c * blk))
            self._units["rows"].append((w2_hbm.at[0, pl.ds(c * blk, blk), :], w2_bf,
                                        (slice(c * blk, (c + 1) * blk), slice(None))))
        self._n_ffn_cols = len(self._units["cols"])
        for b in PROJ_BLOCK_ORDER:
            self._units["cols"].append(col_unit(win_hbm, win_bf, b * blk))

    def _copy(self, kind, k):
        n_slots = self._stage[kind].shape[0]
        src, _, _ = self._units[kind][k]
        return pltpu.make_async_copy(src, self._stage[kind].at[k % n_slots], self._sem[kind].at[k % n_slots])

    def start_all(self):
        for kind in ("cols", "rows"):
            for k in range(min(self._stage[kind].shape[0], len(self._units[kind]))):
                self._copy(kind, k).start()

    def _land(self, kind, upto):
        n_slots = self._stage[kind].shape[0]
        for k in range(self._done[kind], upto):
            self._copy(kind, k).wait()
            _, dst, idx = self._units[kind][k]
            dst[idx] = self._stage[kind][k % n_slots].astype(BF16)
            if k + n_slots < len(self._units[kind]):
                self._copy(kind, k + n_slots).start()
        self._done[kind] = max(self._done[kind], upto)

    def before_ffn_chunk(self, c):
        self._land("cols", 2 * (c + 1))
        self._land("rows", c)

    def before_ffn_out(self):
        self._land("rows", N_FF_CHUNKS)

    def before_proj_blocks(self, n_used):
        self._land("cols", self._n_ffn_cols + n_used)

    def finished(self):
        return all(self._done[kind] == len(self._units[kind]) for kind in self._units)


def _front_compute(x, g1_ref, w1_bf, w2_bf, gm_ref, win_bf, qn_ref, kn_ref,
                   x1_ref, h_ref, u_ref, q_ref, kv_ref, a_scr, casts, loader):
    x1 = _ffn(x, g1_ref, w1_bf, w2_bf, a_scr, casts, loader)
    x1_ref[...] = x1
    xg = x1 * gm_ref[...]
    r = _rms_scale(x1)
    h_ref[...] = (xg * r).astype(BF16)
    h = xg.astype(BF16)
    qg = qn_ref[...] * (HEAD_DIM ** -0.5 * LOG2E)
    blk = 2 * LANES
    for b in range(D_Q // blk):
        if loader is not None:
            loader.before_proj_blocks(b + 1)
        zq = _dot(h, win_bf[:, D_POOL + b * blk:D_POOL + (b + 1) * blk]) * r
        for p in range(2):
            q_ref[:, b * blk + p * LANES:b * blk + (p + 1) * LANES] = _head_norm(
                zq[:, p * LANES:(p + 1) * LANES], qg).astype(BF16)
    if loader is not None:
        loader.before_proj_blocks(D_Q // blk + 1)
    zkv = _dot(h, win_bf[:, D_POOL + D_Q:D_POOL + D_Q + 2 * D_KV]) * r
    kv_ref[:, :D_KV] = _head_norm(zkv[:, :D_KV], kn_ref[...])
    kv_ref[:, D_KV:] = zkv[:, D_KV:]
    if loader is not None:
        loader.before_proj_blocks(len(PROJ_BLOCK_ORDER))
    u_ref[...] = _dot(h, win_bf[:, :D_POOL]) * r


def _front_body(*refs, n_cast, n_prompt_steps):
    xp_ref, xs_ref, g1_ref, w1_hbm, w2_hbm, gm_ref, win_hbm, qn_ref, kn_ref = refs[:9]
    cast_src = refs[9:9 + n_cast]
    outs = refs[9 + n_cast:14 + n_cast]
    cast_dst = refs[14 + n_cast:14 + 2 * n_cast]
    a_scr, w1_bf, w2_bf, win_bf, stage_cols, stage_rows, sem_cols, sem_rows = refs[14 + 2 * n_cast:]
    casts = [functools.partial(_cast_block, src, dst) for src, dst in zip(cast_src, cast_dst)]
    i = pl.program_id(0)
    x = jnp.where(i < n_prompt_steps, xp_ref[...], xs_ref[...])
    weights = (g1_ref, w1_bf, w2_bf, gm_ref, win_bf, qn_ref, kn_ref)

    @pl.when(i == 0)
    def _():
        loader = _WeightLoader(w1_hbm, w2_hbm, win_hbm, w1_bf, w2_bf, win_bf,
                               stage_cols, stage_rows, sem_cols, sem_rows)
        loader.start_all()
        _front_compute(x, *weights, *outs, a_scr, casts, loader)
        assert loader.finished()

    @pl.when(i > 0)
    def _():
        _front_compute(x, *weights, *outs, a_scr, casts, None)


def _pool_pair(ext_ref, take_new, first_row, pw_ref, pb_ref, ps_ref, pair):
    pooled = []
    for gi in (2 * pair, 2 * pair + 1):
        w = POOL_WINDOWS[gi]
        ext = ext_ref[:, gi * POOL_GROUP:(gi + 1) * POOL_GROUP]
        u_new = take_new(ext)
        s = ext
        k = 1
        while k < w:
            s = s + pltpu.roll(s, k, axis=0)
            k *= 2
        s = take_new(s)
        if first_row is None:
            mean = s * (1.0 / w)
        else:
            pos = first_row + lax.broadcasted_iota(jnp.int32, (s.shape[0], 1), 0)
            mean = s * (1.0 / jnp.minimum(pos + 1, w).astype(F32))
        pooled.append((mean - u_new).astype(BF16))
    cols = slice(2 * pair * POOL_GROUP, 2 * (pair + 1) * POOL_GROUP)
    return (_dot(jnp.concatenate(pooled, axis=-1), pw_ref[pair]) + pb_ref[:, cols]) * ps_ref[:, cols]


def _pool_mixer(ext_ref, take_new, first_row, pw_ref, pb_ref, ps_ref):
    return jnp.concatenate([_pool_pair(ext_ref, take_new, first_row, pw_ref, pb_ref, ps_ref, i).astype(BF16)
                            for i in range(N_POOL_GROUPS // 2)], axis=-1)


def _fill_kv_variants(kx_ref, vx_ref, row0, kv):
    rows = kv.shape[0]
    low = lax.broadcasted_iota(jnp.int32, (rows, LANES), 1) < HEAD_DIM
    ones = (jnp.where(low, 1.0, 0.0).astype(BF16), jnp.where(low, 0.0, 1.0).astype(BF16))
    for t, base in ((0, 0), (1, D_KV)):
        both = kv[:, base:base + D_KV]
        swapped = pltpu.roll(both, HEAD_DIM, axis=1)
        variants = (jnp.where(low, both, 0.0), jnp.where(low, 0.0, swapped),
                    jnp.where(low, swapped, 0.0), jnp.where(low, 0.0, both))
        for i, val in enumerate(variants):
            if t == 0:
                kx_ref[i, row0:row0 + rows, :] = val.astype(BF16)
            else:
                vx_ref[i, row0:row0 + rows, :LANES] = val.astype(BF16)
                vx_ref[i, row0:row0 + rows, LANES:] = ones[i % 2]


def _sink_columns(sinks_ref, q_rows):
    m_rows = PAIRS_PER_KV * q_rows
    row_blk = lax.broadcasted_iota(jnp.int32, (m_rows, 1), 0) // q_rows
    cols = []
    for kh in range(N_KV_HEADS):
        pair = []
        for ab in range(2):
            sink = jnp.zeros((m_rows, 1), F32)
            for pp in range(PAIRS_PER_KV):
                sink = jnp.where(row_blk == pp, sinks_ref[kh * 2 * PAIRS_PER_KV + 2 * pp + ab] * LOG2E, sink)
            pair.append(sink)
        cols.append(pair)
    return cols


def _attend_block(q_ref, q_row0, q_rows, kx_ref, vx_ref, key_row0, n_keys, key_valid, sink_cols, yb_ref):
    m_rows = PAIRS_PER_KV * q_rows
    low = lax.broadcasted_iota(jnp.int32, (m_rows, LANES), 1) < HEAD_DIM
    for kh in range(N_KV_HEADS):
        qs = jnp.concatenate(
            [q_ref[q_row0:q_row0 + q_rows, (kh * PAIRS_PER_KV + pp) * LANES:(kh * PAIRS_PER_KV + pp + 1) * LANES]
             for pp in range(PAIRS_PER_KV)], axis=0)
        o = None
        row_max = []
        for ab in range(2):
            s = _dot_nt(qs, kx_ref[2 * kh + ab, key_row0:key_row0 + n_keys, :])
            if key_valid is not None:
                s = jnp.where(key_valid, s, NEG_BIG)
            m = jnp.maximum(jnp.max(s, axis=-1, keepdims=True), sink_cols[kh][ab])
            p = jnp.exp2(s - m).astype(BF16)
            pv = _dot(p, vx_ref[2 * kh + ab, key_row0:key_row0 + n_keys, :])
            o = pv if o is None else o + pv
            row_max.append(m)
        sink_p = jnp.exp2(jnp.where(low, sink_cols[kh][0], sink_cols[kh][1])
                          - jnp.where(low, row_max[0], row_max[1]))
        out = o[:, :LANES] / (o[:, LANES:] + sink_p)
        for pp in range(PAIRS_PER_KV):
            col = (kh * PAIRS_PER_KV + pp) * LANES
            yb_ref[q_row0:q_row0 + q_rows, col:col + LANES] = out[pp * q_rows:(pp + 1) * q_rows].astype(BF16)


def _dense_tail(x1, h, ya, yb, wg_ref, bg_ref, wpp_ref, wap_ref, wo_ref, g2_ref, w1_ref, w2_ref, a_scr):
    gates = jax.nn.sigmoid(_dot(h, wg_ref[...]) + bg_ref[...])
    a = _dot(ya, wpp_ref[...])
    b = _dot(yb, wap_ref[...])
    mixed = gates[:, :D_MODEL] * a + gates[:, D_MODEL:] * b
    x2 = x1 + _dot(mixed.astype(BF16), wo_ref[...])
    return _ffn(x2, g2_ref, w1_ref, w2_ref, a_scr)


def _prompt_mix_body(sinks_ref, x1_ref, h_ref, u_ref, q_ref, kv_ref,
                     pw_ref, pb_ref, ps_ref, wg_ref, bg_ref, wpp_ref, wap_ref, wo_ref,
                     g2_ref, w1_ref, w2_ref,
                     o_ref, uext, kx, vx, yb_scr, a_scr, *, rows):
    t = pl.program_id(1)

    @pl.when(t == 0)
    def _():
        uext[:POOL_HIST_PAD, :] = jnp.zeros((POOL_HIST_PAD, D_POOL), F32)
        kx[:, :WINDOW, :] = jnp.zeros((4, WINDOW, LANES), BF16)
        vx[:, :WINDOW, :] = jnp.zeros((4, WINDOW, 2 * LANES), BF16)

    uext[POOL_HIST_PAD:, :] = u_ref[...]
    _fill_kv_variants(kx, vx, WINDOW, kv_ref[...])

    ya = _pool_mixer(uext, lambda a: a[POOL_HIST_PAD:], t * rows, pw_ref, pb_ref, ps_ref)

    n_chunks = rows // CHUNK
    n_keys = WINDOW + CHUNK
    key_chunk = lax.broadcasted_iota(jnp.int32, (1, n_keys), 1) // CHUNK
    sink_cols = _sink_columns(sinks_ref, CHUNK)
    for j in range(n_chunks):
        first_chunk = t * n_chunks + j - WINDOW // CHUNK
        key_valid = key_chunk + first_chunk >= 0 if j < WINDOW // CHUNK else None
        _attend_block(q_ref, j * CHUNK, CHUNK, kx, vx, j * CHUNK, n_keys, key_valid, sink_cols, yb_scr)

    o_ref[...] = _dense_tail(x1_ref[...], h_ref[...], ya, yb_scr[...], wg_ref, bg_ref, wpp_ref, wap_ref, wo_ref,
                             g2_ref, w1_ref, w2_ref, a_scr)

    uext[:POOL_HIST_PAD, :] = uext[rows:rows + POOL_HIST_PAD, :]
    kx[:, :WINDOW, :] = kx[:, rows:rows + WINDOW, :]
    vx[:, :WINDOW, :] = vx[:, rows:rows + WINDOW, :]


def _sample_tail_body(sinks_ref, x1_ref, h_ref, u_ref, q_ref, kv_ref, hu_ref, hkv_ref,
                      pw_ref, pb_ref, ps_ref, wg_ref, bg_ref, wpp_ref, wap_ref, wo_ref,
                      g2_ref, w1_ref, w2_ref,
                      o_ref, uext, kx, vx, yb_scr, a_scr, *, n_streams, t_new):
    seg = POOL_HIST_PAD + t_new
    for e in range(n_streams):
        uext[e * seg:e * seg + POOL_HIST_PAD, :] = hu_ref[e]
        uext[e * seg + POOL_HIST_PAD:(e + 1) * seg, :] = u_ref[e * t_new:(e + 1) * t_new, :]

    def take_new(a):
        return jnp.concatenate([a[e * seg + POOL_HIST_PAD:(e + 1) * seg] for e in range(n_streams)], axis=0)

    ya = _pool_mixer(uext, take_new, None, pw_ref, pb_ref, ps_ref)

    sink_cols = _sink_columns(sinks_ref, t_new)
    for e in range(n_streams):
        kx_e, vx_e = kx.at[e % 2], vx.at[e % 2]
        _fill_kv_variants(kx_e, vx_e, 0, hkv_ref[e])
        _fill_kv_variants(kx_e, vx_e, WINDOW, kv_ref[e * t_new:(e + 1) * t_new, :])
        _attend_block(q_ref, e * t_new, t_new, kx_e, vx_e, 0, WINDOW + t_new, None, sink_cols, yb_scr)

    o_ref[...] = _dense_tail(x1_ref[...], h_ref[...], ya, yb_scr[...], wg_ref, bg_ref, wpp_ref, wap_ref, wo_ref,
                             g2_ref, w1_ref, w2_ref, a_scr)


def _const_spec(shape):
    return pl.BlockSpec(shape, lambda *_: (0,) * len(shape), pipeline_mode=pl.Buffered(1))


def _front_call(xp2d, xs2d, g1, w1, w2, gm, win, qn2, kn2, cast_srcs):
    n_p = xp2d.shape[0]
    tm = ROW_TILE
    assert n_p % tm == 0 and xs2d.shape[0] == tm
    n_prompt_steps = n_p // tm
    steps = n_prompt_steps + 1
    n = n_p + tm
    row = lambda width: pl.BlockSpec((tm, width), lambda i: (i, 0))
    cast_in, cast_out, cast_shapes = [], [], []
    for w, cols in cast_srcs:
        _, w_rows, w_cols = w.shape
        n_blk = max(d for d in range(1, steps + 1) if w_rows % (BF16_SUBLANES * d) == 0)
        rows = w_rows // n_blk
        assert cols % LANES == 0
        cast_in.append(pl.BlockSpec((None, rows, w_cols), lambda i, n_blk=n_blk: (0, jnp.minimum(i, n_blk - 1), 0)))
        cast_out.append(pl.BlockSpec((rows, cols), lambda i, n_blk=n_blk: (jnp.minimum(i, n_blk - 1), 0)))
        cast_shapes.append(jax.ShapeDtypeStruct((w_rows, cols), BF16))
    hbm = pl.BlockSpec(memory_space=pl.ANY)
    col_slots, row_slots = WEIGHT_STAGE_SLOTS
    return pl.pallas_call(
        functools.partial(_front_body, n_cast=len(cast_srcs), n_prompt_steps=n_prompt_steps),
        grid=(steps,),
        in_specs=[pl.BlockSpec((tm, D_MODEL), lambda i: (jnp.minimum(i, n_prompt_steps - 1), 0)),
                  _const_spec(xs2d.shape),
                  _const_spec(g1.shape), hbm, hbm, _const_spec(gm.shape), hbm,
                  _const_spec(qn2.shape), _const_spec(kn2.shape)]
                 + cast_in,
        out_specs=[row(D_MODEL), row(D_MODEL), row(D_POOL), row(D_Q), row(2 * D_KV)] + cast_out,
        out_shape=[jax.ShapeDtypeStruct((n, D_MODEL), F32), jax.ShapeDtypeStruct((n, D_MODEL), BF16),
                   jax.ShapeDtypeStruct((n, D_POOL), F32), jax.ShapeDtypeStruct((n, D_Q), BF16),
                   jax.ShapeDtypeStruct((n, 2 * D_KV), F32)] + cast_shapes,
        scratch_shapes=[pltpu.VMEM((tm, D_FF), BF16),
                        pltpu.VMEM((D_MODEL, 2 * D_FF), BF16), pltpu.VMEM((D_FF, D_MODEL), BF16),
                        pltpu.VMEM((D_MODEL, D_UQKV), BF16),
                        pltpu.VMEM((col_slots, D_MODEL, FF_CHUNK), F32), pltpu.VMEM((row_slots, FF_CHUNK, D_MODEL), F32),
                        pltpu.SemaphoreType.DMA((col_slots,)), pltpu.SemaphoreType.DMA((row_slots,))],
        compiler_params=pltpu.CompilerParams(dimension_semantics=("arbitrary",),
                                             vmem_limit_bytes=VMEM_LIMIT_BYTES),
        name="front",
    )(xp2d, xs2d, g1, w1, w2, gm, win, qn2, kn2, *[w for w, _ in cast_srcs])


def _prompt_mix_call(sinks, x1, h, u, q, kv, mix_w, ffn_w, n_seq, seq_len):
    tm = ROW_TILE
    assert seq_len % tm == 0
    tiles = seq_len // tm
    row = lambda width: pl.BlockSpec((tm, width), lambda b, t: (b * tiles + t, 0))
    weights = tuple(mix_w) + tuple(ffn_w)
    return pl.pallas_call(
        functools.partial(_prompt_mix_body, rows=tm),
        grid=(n_seq, tiles),
        in_specs=[pl.BlockSpec(memory_space=pltpu.SMEM),
                  row(D_MODEL), row(D_MODEL), row(D_POOL), row(D_Q), row(2 * D_KV)]
                 + [_const_spec(w.shape) for w in weights],
        out_specs=row(D_MODEL),
        out_shape=jax.ShapeDtypeStruct((n_seq * seq_len, D_MODEL), F32),
        scratch_shapes=[pltpu.VMEM((POOL_HIST_PAD + tm, D_POOL), F32),
                        pltpu.VMEM((4, WINDOW + tm, LANES), BF16),
                        pltpu.VMEM((4, WINDOW + tm, 2 * LANES), BF16),
                        pltpu.VMEM((tm, D_Q), BF16),
                        pltpu.VMEM((tm, D_FF), BF16)],
        compiler_params=pltpu.CompilerParams(dimension_semantics=("arbitrary", "arbitrary"),
                                             vmem_limit_bytes=VMEM_LIMIT_BYTES),
        name="prompt_mix",
    )(sinks, x1, h, u, q, kv, *weights)


def _sample_tail_call(sinks, x1, h, u, q, kv, hist_u, hist_kv, mix_w, ffn_w, row_block, n_streams, t_new):
    tm = ROW_TILE
    assert n_streams * t_new == tm and t_new % BF16_SUBLANES == 0
    row = lambda width: pl.BlockSpec((tm, width), lambda i: (row_block, 0))
    weights = tuple(mix_w) + tuple(ffn_w)
    return pl.pallas_call(
        functools.partial(_sample_tail_body, n_streams=n_streams, t_new=t_new),
        grid=(1,),
        in_specs=[pl.BlockSpec(memory_space=pltpu.SMEM),
                  row(D_MODEL), row(D_MODEL), row(D_POOL), row(D_Q), row(2 * D_KV),
                  _const_spec(hist_u.shape), _const_spec(hist_kv.shape)]
                 + [_const_spec(w.shape) for w in weights],
        out_specs=pl.BlockSpec((tm, D_MODEL), lambda i: (0, 0)),
        out_shape=jax.ShapeDtypeStruct((tm, D_MODEL), F32),
        scratch_shapes=[pltpu.VMEM((n_streams * (POOL_HIST_PAD + t_new), D_POOL), F32),
                        pltpu.VMEM((2, 4, WINDOW + t_new, LANES), BF16),
                        pltpu.VMEM((2, 4, WINDOW + t_new, 2 * LANES), BF16),
                        pltpu.VMEM((tm, D_Q), BF16),
                        pltpu.VMEM((tm, D_FF), BF16)],
        compiler_params=pltpu.CompilerParams(dimension_semantics=("arbitrary",),
                                             vmem_limit_bytes=VMEM_LIMIT_BYTES),
        name="sample_tail",
    )(sinks, x1, h, u, q, kv, hist_u, hist_kv, *weights)


def kernel(x_prompt, x_sample, state_pool, cache_k, cache_v, norm_ffn1, ffn1_w_in, ffn1_w_out, norm_mix, w_in, b_gate, pool_w, pool_b, pool_scale, q_norm, k_norm, sinks, w_pool_proj, w_attn_proj, w_out, norm_ffn2, ffn2_w_in, ffn2_w_out):
    depth = norm_ffn1.shape[0]
    assert depth == 1, "single-layer trunk"
    bp, sp, _ = x_prompt.shape
    bs, ts, _ = x_sample.shape
    n_p = bp * sp

    front_w = (norm_ffn1[0].reshape(1, D_MODEL), ffn1_w_in, ffn1_w_out, norm_mix[0].reshape(1, D_MODEL), w_in,
               jnp.tile(q_norm[0], 2).reshape(1, LANES), jnp.tile(k_norm[0], 2).reshape(1, LANES))
    cast_srcs = ((w_in, 2 * D_MODEL), (w_pool_proj, D_MODEL), (w_attn_proj, D_MODEL), (w_out, D_MODEL),
                 (ffn2_w_in, 2 * D_FF), (ffn2_w_out, D_MODEL))
    x1, h, u, q, kv, wg, wpp, wap, wo, ffn2_w1, ffn2_w2 = _front_call(
        x_prompt.reshape(n_p, D_MODEL), x_sample.reshape(bs * ts, D_MODEL), *front_w, cast_srcs=cast_srcs)

    ffn2 = (norm_ffn2[0].reshape(1, D_MODEL), ffn2_w1, ffn2_w2)
    pw = pool_w[0].astype(BF16)
    zero_blk = jnp.zeros((POOL_GROUP, POOL_GROUP), BF16)
    pw_pairs = jnp.stack([jnp.block([[pw[2 * i], zero_blk], [zero_blk, pw[2 * i + 1]]])
                          for i in range(N_POOL_GROUPS // 2)])
    mix_w = (pw_pairs, pool_b[0].reshape(1, D_POOL), pool_scale[0].reshape(1, D_POOL),
             wg, b_gate[0].reshape(1, 2 * D_MODEL), wpp, wap, wo)
    sink_vec = sinks[0]

    y_prompt = _prompt_mix_call(sink_vec, x1, h, u, q, kv, mix_w, ffn2, bp, sp).reshape(bp, sp, D_MODEL)
    new_pool_p = jnp.stack([u[(b + 1) * sp - POOL_HIST:(b + 1) * sp] for b in range(bp)])
    kv_last = jnp.stack([kv[(b + 1) * sp - WINDOW:(b + 1) * sp] for b in range(bp)])
    new_k_p = kv_last[..., :D_KV].reshape(bp, WINDOW, N_KV_HEADS, HEAD_DIM)
    new_v_p = kv_last[..., D_KV:].reshape(bp, WINDOW, N_KV_HEADS, HEAD_DIM)

    hist_u = jnp.pad(state_pool[0], ((0, 0), (POOL_HIST_PAD - POOL_HIST, 0), (0, 0)))
    hist_kv = jnp.concatenate([cache_k[0].reshape(bs, WINDOW, D_KV), cache_v[0].reshape(bs, WINDOW, D_KV)], axis=-1)
    y_sample = _sample_tail_call(sink_vec, x1, h, u, q, kv, hist_u, hist_kv, mix_w, ffn2,
                                 n_p // ROW_TILE, bs, ts).reshape(bs, ts, D_MODEL)
    us3 = u[n_p:].reshape(bs, ts, D_POOL)
    kvs3 = kv[n_p:].reshape(bs, ts, 2 * D_KV)
    new_pool_s = jnp.concatenate([state_pool[0], us3], axis=1)[:, -POOL_HIST:]
    kv_full = jnp.concatenate([hist_kv, kvs3], axis=1)[:, -WINDOW:]
    new_k_s = kv_full[..., :D_KV].reshape(bs, WINDOW, N_KV_HEADS, HEAD_DIM)
    new_v_s = kv_full[..., D_KV:].reshape(bs, WINDOW, N_KV_HEADS, HEAD_DIM)

    return (y_prompt, y_sample, new_pool_p[None], new_k_p[None], new_v_p[None],
            new_pool_s[None], new_k_s[None], new_v_s[None])
```

```python
import functools

import jax
import jax.numpy as jnp
from jax import lax
from jax.experimental import pallas as pl
from jax.experimental.pallas import tpu as pltpu

D_MODEL = 1024
CHUNK = 64
N_HEADS = 16
N_KV_HEADS = 2
HEAD_DIM = 64
WINDOW = 128
D_POOL = 512
N_POOL_GROUPS = 4
POOL_GROUP = 128
POOL_WINDOWS = (2, 4, 8, 16)
POOL_HIST = 15
POOL_HIST_PAD = 16
D_Q = N_HEADS * HEAD_DIM
D_KV = N_KV_HEADS * HEAD_DIM
D_UQKV = D_POOL + D_Q + 2 * D_KV
D_FF = 2816
EPS = 1e-6

LANES = 128
BF16_SUBLANES = 16
FF_CHUNK = 256
N_FF_CHUNKS = D_FF // FF_CHUNK
PAIRS_PER_KV = (N_HEADS // N_KV_HEADS) // 2
ROW_TILE = 512
PROJ_BLOCK_ORDER = (2, 3, 4, 5, 6, 0, 1)
WEIGHT_STAGE_SLOTS = (5, 2)
VMEM_LIMIT_BYTES = 58 * 1024 * 1024
NEG_BIG = float(jnp.finfo(jnp.float32).min)
LOG2E = 1.4426950408889634

BF16 = jnp.bfloat16
F32 = jnp.float32


def _dot(a, b):
    return jnp.dot(a, b, preferred_element_type=F32)


def _dot_nt(a, b):
    return lax.dot_general(a, b, (((1,), (1,)), ((), ())), preferred_element_type=F32)


def _rms_scale(x):
    return lax.rsqrt(jnp.mean(x * x, axis=-1, keepdims=True) + EPS)


def _ffn(x, g_ref, w1_ref, w2_ref, a_scr, side_work=(), loader=None):
    xg = (x * g_ref[...]).astype(BF16)
    r = _rms_scale(x)
    for c in range(N_FF_CHUNKS):
        if loader is not None:
            loader.before_ffn_chunk(c)
        gate = _dot(xg, w1_ref[:, c * FF_CHUNK:(c + 1) * FF_CHUNK]) * r
        up = _dot(xg, w1_ref[:, D_FF + c * FF_CHUNK:D_FF + (c + 1) * FF_CHUNK]) * r
        a_scr[:, c * FF_CHUNK:(c + 1) * FF_CHUNK] = (gate * jax.nn.sigmoid(gate) * up).astype(BF16)
        if c < len(side_work):
            side_work[c]()
    assert len(side_work) <= N_FF_CHUNKS
    if loader is not None:
        loader.before_ffn_out()
    return x + 0.5 * _dot(a_scr[...], w2_ref[...])


def _half_sumsq(x):
    y = x * x
    low = lax.broadcasted_iota(jnp.int32, y.shape, 1) < HEAD_DIM
    sum_low = jnp.sum(jnp.where(low, y, 0.0), axis=-1, keepdims=True)
    sum_high = jnp.sum(jnp.where(low, 0.0, y), axis=-1, keepdims=True)
    return jnp.where(low, sum_low, sum_high)


def _head_norm(x, g2):
    return x * lax.rsqrt(_half_sumsq(x) * (1.0 / HEAD_DIM) + EPS) * g2


def _cast_block(src_ref, dst_ref):
    dst_ref[...] = src_ref[:, src_ref.shape[1] - dst_ref.shape[1]:].astype(BF16)


class _WeightLoader:
    def __init__(self, w1_hbm, w2_hbm, win_hbm, w1_bf, w2_bf, win_bf, stage_cols, stage_rows, sem_cols, sem_rows):
        blk = FF_CHUNK
        self._units = {"cols": [], "rows": []}
        self._stage = {"cols": stage_cols, "rows": stage_rows}
        self._sem = {"cols": sem_cols, "rows": sem_rows}
        self._done = {"cols": 0, "rows": 0}

        def col_unit(hbm, dst, col):
            return (hbm.at[0, :, pl.ds(col, blk)], dst, (slice(None), slice(col, col + blk)))

        for c in range(N_FF_CHUNKS):
            self._units["cols"].append(col_unit(w1_hbm, w1_bf, c * blk))
            self._units["cols"].append(col_unit(w1_hbm, w1_bf, D_FF + c * blk))
            self._units["rows"].append((w2_hbm.at[0, pl.ds(c * blk, blk), :], w2_bf,
                                        (slice(c * blk, (c + 1) * blk), slice(None))))
        self._n_ffn_cols = len(self._units["cols"])
        for b in PROJ_BLOCK_ORDER:
            self._units["cols"].append(col_unit(win_hbm, win_bf, b * blk))

    def _copy(self, kind, k):
        n_slots = self._stage[kind].shape[0]
        src, _, _ = self._units[kind][k]
        return pltpu.make_async_copy(src, self._stage[kind].at[k % n_slots], self._sem[kind].at[k % n_slots])

    def start_all(self):
        for kind in ("cols", "rows"):
            for k in range(min(self._stage[kind].shape[0], len(self._units[kind]))):
                self._copy(kind, k).start()

    def _land(self, kind, upto):
        n_slots = self._stage[kind].shape[0]
        for k in range(self._done[kind], upto):
            self._copy(kind, k).wait()
            _, dst, idx = self._units[kind][k]
            dst[idx] = self._stage[kind][k % n_slots].astype(BF16)
            if k + n_slots < len(self._units[kind]):
                self._copy(kind, k + n_slots).start()
        self._done[kind] = max(self._done[kind], upto)

    def before_ffn_chunk(self, c):
        self._land("cols", 2 * (c + 1))
        self._land("rows", c)

    def before_ffn_out(self):
        self._land("rows", N_FF_CHUNKS)

    def before_proj_blocks(self, n_used):
        self._land("cols", self._n_ffn_cols + n_used)

    def finished(self):
        return all(self._done[kind] == len(self._units[kind]) for kind in self._units)


def _front_compute(x, g1_ref, w1_bf, w2_bf, gm_ref, win_bf, qn_ref, kn_ref,
                   x1_ref, h_ref, u_ref, q_ref, kv_ref, a_scr, casts, loader):
    x1 = _ffn(x, g1_ref, w1_bf, w2_bf, a_scr, casts, loader)
    x1_ref[...] = x1
    xg = x1 * gm_ref[...]
    r = _rms_scale(x1)
    h_ref[...] = (xg * r).astype(BF16)
    h = xg.astype(BF16)
    qg = qn_ref[...] * (HEAD_DIM ** -0.5 * LOG2E)
    blk = 2 * LANES
    for b in range(D_Q // blk):
        if loader is not None:
            loader.before_proj_blocks(b + 1)
        zq = _dot(h, win_bf[:, D_POOL + b * blk:D_POOL + (b + 1) * blk]) * r
        for p in range(2):
            q_ref[:, b * blk + p * LANES:b * blk + (p + 1) * LANES] = _head_norm(
                zq[:, p * LANES:(p + 1) * LANES], qg).astype(BF16)
    if loader is not None:
        loader.before_proj_blocks(D_Q // blk + 1)
    zkv = _dot(h, win_bf[:, D_POOL + D_Q:D_POOL + D_Q + 2 * D_KV]) * r
    kv_ref[:, :D_KV] = _head_norm(zkv[:, :D_KV], kn_ref[...])
    kv_ref[:, D_KV:] = zkv[:, D_KV:]
    if loader is not None:
        loader.before_proj_blocks(len(PROJ_BLOCK_ORDER))
    u_ref[...] = _dot(h, win_bf[:, :D_POOL]) * r


def _front_body(*refs, n_cast, n_prompt_steps):
    xp_ref, xs_ref, g1_ref, w1_hbm, w2_hbm, gm_ref, win_hbm, qn_ref, kn_ref = refs[:9]
    cast_src = refs[9:9 + n_cast]
    outs = refs[9 + n_cast:14 + n_cast]
    cast_dst = refs[14 + n_cast:14 + 2 * n_cast]
    a_scr, w1_bf, w2_bf, win_bf, stage_cols, stage_rows, sem_cols, sem_rows = refs[14 + 2 * n_cast:]
    casts = [functools.partial(_cast_block, src, dst) for src, dst in zip(cast_src, cast_dst)]
    i = pl.program_id(0)
    x = jnp.where(i < n_prompt_steps, xp_ref[...], xs_ref[...])
    weights = (g1_ref, w1_bf, w2_bf, gm_ref, win_bf, qn_ref, kn_ref)

    @pl.when(i == 0)
    def _():
        loader = _WeightLoader(w1_hbm, w2_hbm, win_hbm, w1_bf, w2_bf, win_bf,
                               stage_cols, stage_rows, sem_cols, sem_rows)
        loader.start_all()
        _front_compute(x, *weights, *outs, a_scr, casts, loader)
        assert loader.finished()

    @pl.when(i > 0)
    def _():
        _front_compute(x, *weights, *outs, a_scr, casts, None)


def _pool_pair(ext_ref, take_new, first_row, pw_ref, pb_ref, ps_ref, pair):
    pooled = []
    for gi in (2 * pair, 2 * pair + 1):
        w = POOL_WINDOWS[gi]
        ext = ext_ref[:, gi * POOL_GROUP:(gi + 1) * POOL_GROUP]
        u_new = take_new(ext)
        s = ext
        k = 1
        while k < w:
            s = s + pltpu.roll(s, k, axis=0)
            k *= 2
        s = take_new(s)
        if first_row is None:
            mean = s * (1.0 / w)
        else:
            pos = first_row + lax.broadcasted_iota(jnp.int32, (s.shape[0], 1), 0)
            mean = s * (1.0 / jnp.minimum(pos + 1, w).astype(F32))
        pooled.append((mean - u_new).astype(BF16))
    cols = slice(2 * pair * POOL_GROUP, 2 * (pair + 1) * POOL_GROUP)
    return (_dot(jnp.concatenate(pooled, axis=-1), pw_ref[pair]) + pb_ref[:, cols]) * ps_ref[:, cols]


def _pool_mixer(ext_ref, take_new, first_row, pw_ref, pb_ref, ps_ref):
    return jnp.concatenate([_pool_pair(ext_ref, take_new, first_row, pw_ref, pb_ref, ps_ref, i).astype(BF16)
                            for i in range(N_POOL_GROUPS // 2)], axis=-1)


def _fill_kv_variants(kx_ref, vx_ref, row0, kv):
    rows = kv.shape[0]
    low = lax.broadcasted_iota(jnp.int32, (rows, LANES), 1) < HEAD_DIM
    ones = (jnp.where(low, 1.0, 0.0).astype(BF16), jnp.where(low, 0.0, 1.0).astype(BF16))
    for t, base in ((0, 0), (1, D_KV)):
        both = kv[:, base:base + D_KV]
        swapped = pltpu.roll(both, HEAD_DIM, axis=1)
        variants = (jnp.where(low, both, 0.0), jnp.where(low, 0.0, swapped),
                    jnp.where(low, swapped, 0.0), jnp.where(low, 0.0, both))
        for i, val in enumerate(variants):
            if t == 0:
                kx_ref[i, row0:row0 + rows, :] = val.astype(BF16)
            else:
                vx_ref[i, row0:row0 + rows, :LANES] = val.astype(BF16)
                vx_ref[i, row0:row0 + rows, LANES:] = ones[i % 2]


def _sink_columns(sinks_ref, q_rows):
    m_rows = PAIRS_PER_KV * q_rows
    row_blk = lax.broadcasted_iota(jnp.int32, (m_rows, 1), 0) // q_rows
    cols = []
    for kh in range(N_KV_HEADS):
        pair = []
        for ab in range(2):
            sink = jnp.zeros((m_rows, 1), F32)
            for pp in range(PAIRS_PER_KV):
                sink = jnp.where(row_blk == pp, sinks_ref[kh * 2 * PAIRS_PER_KV + 2 * pp + ab] * LOG2E, sink)
            pair.append(sink)
        cols.append(pair)
    return cols


def _attend_block(q_ref, q_row0, q_rows, kx_ref, vx_ref, key_row0, n_keys, key_valid, sink_cols, yb_ref):
    m_rows = PAIRS_PER_KV * q_rows
    low = lax.broadcasted_iota(jnp.int32, (m_rows, LANES), 1) < HEAD_DIM
    for kh in range(N_KV_HEADS):
        qs = jnp.concatenate(
            [q_ref[q_row0:q_row0 + q_rows, (kh * PAIRS_PER_KV + pp) * LANES:(kh * PAIRS_PER_KV + pp + 1) * LANES]
             for pp in range(PAIRS_PER_KV)], axis=0)
        o = None
        row_max = []
        for ab in range(2):
            s = _dot_nt(qs, kx_ref[2 * kh + ab, key_row0:key_row0 + n_keys, :])
            if key_valid is not None:
                s = jnp.where(key_valid, s, NEG_BIG)
            m = jnp.maximum(jnp.max(s, axis=-1, keepdims=True), sink_cols[kh][ab])
            p = jnp.exp2(s - m).astype(BF16)
            pv = _dot(p, vx_ref[2 * kh + ab, key_row0:key_row0 + n_keys, :])
            o = pv if o is None else o + pv
            row_max.append(m)
        sink_p = jnp.exp2(jnp.where(low, sink_cols[kh][0], sink_cols[kh][1])
                          - jnp.where(low, row_max[0], row_max[1]))
        out = o[:, :LANES] / (o[:, LANES:] + sink_p)
        for pp in range(PAIRS_PER_KV):
            col = (kh * PAIRS_PER_KV + pp) * LANES
            yb_ref[q_row0:q_row0 + q_rows, col:col + LANES] = out[pp * q_rows:(pp + 1) * q_rows].astype(BF16)


DENSE_WEIGHTS = ("wg", "wpp", "wap", "wo", "w1", "w2")


class _LazyWeights:
    def __init__(self, hbm_refs, vmem_refs, sem):
        assert len(hbm_refs) == len(vmem_refs) == len(DENSE_WEIGHTS)
        self._copies = [pltpu.make_async_copy(src, dst, sem.at[k])
                        for k, (src, dst) in enumerate(zip(hbm_refs, vmem_refs))]
        self._waited = set()

    def start_all(self):
        for copy in self._copies:
            copy.start()

    def wait(self, name):
        k = DENSE_WEIGHTS.index(name)
        if k not in self._waited:
            self._copies[k].wait()
            self._waited.add(k)

    def before_ffn_chunk(self, c):
        if c == 0:
            self.wait("w1")

    def before_ffn_out(self):
        self.wait("w2")

    def finished(self):
        return len(self._waited) == len(self._copies)


def _dense_tail(x1, h, ya, yb, wg_ref, bg_ref, wpp_ref, wap_ref, wo_ref, g2_ref, w1_ref, w2_ref, a_scr, lazy=None):
    wait = lazy.wait if lazy is not None else (lambda name: None)
    wait("wg")
    gates = jax.nn.sigmoid(_dot(h, wg_ref[...]) + bg_ref[...])
    wait("wpp")
    a = _dot(ya, wpp_ref[...])
    wait("wap")
    b = _dot(yb, wap_ref[...])
    mixed = gates[:, :D_MODEL] * a + gates[:, D_MODEL:] * b
    wait("wo")
    x2 = x1 + _dot(mixed.astype(BF16), wo_ref[...])
    return _ffn(x2, g2_ref, w1_ref, w2_ref, a_scr, loader=lazy)


def _prompt_mix_tile(sinks_ref, x1_ref, h_ref, u_ref, q_ref, kv_ref, pw_ref, pb_ref, ps_ref, bg_ref, g2_ref,
                     dense_w, o_ref, uext, kx, vx, yb_scr, a_scr, lazy):
    rows = x1_ref.shape[0]
    t = pl.program_id(1)
    wg_ref, wpp_ref, wap_ref, wo_ref, w1_ref, w2_ref = dense_w

    @pl.when(t == 0)
    def _():
        uext[:POOL_HIST_PAD, :] = jnp.zeros((POOL_HIST_PAD, D_POOL), F32)
        kx[:, :WINDOW, :] = jnp.zeros((4, WINDOW, LANES), BF16)
        vx[:, :WINDOW, :] = jnp.zeros((4, WINDOW, 2 * LANES), BF16)

    uext[POOL_HIST_PAD:, :] = u_ref[...]
    _fill_kv_variants(kx, vx, WINDOW, kv_ref[...])

    ya = _pool_mixer(uext, lambda a: a[POOL_HIST_PAD:], t * rows, pw_ref, pb_ref, ps_ref)

    n_chunks = rows // CHUNK
    n_keys = WINDOW + CHUNK
    key_chunk = lax.broadcasted_iota(jnp.int32, (1, n_keys), 1) // CHUNK
    sink_cols = _sink_columns(sinks_ref, CHUNK)
    for j in range(n_chunks):
        first_chunk = t * n_chunks + j - WINDOW // CHUNK
        key_valid = key_chunk + first_chunk >= 0 if j < WINDOW // CHUNK else None
        _attend_block(q_ref, j * CHUNK, CHUNK, kx, vx, j * CHUNK, n_keys, key_valid, sink_cols, yb_scr)

    o_ref[...] = _dense_tail(x1_ref[...], h_ref[...], ya, yb_scr[...], wg_ref, bg_ref, wpp_ref, wap_ref, wo_ref,
                             g2_ref, w1_ref, w2_ref, a_scr, lazy)

    uext[:POOL_HIST_PAD, :] = uext[rows:rows + POOL_HIST_PAD, :]
    kx[:, :WINDOW, :] = kx[:, rows:rows + WINDOW, :]
    vx[:, :WINDOW, :] = vx[:, rows:rows + WINDOW, :]


def _prompt_mix_body(*refs):
    n_small = 11
    n_w = len(DENSE_WEIGHTS)
    _prompt_mix_tile(*refs[:n_small], refs[n_small:n_small + n_w], *refs[n_small + n_w:], None)


def _sample_tail_body(*refs, n_streams, t_new):
    n_w = len(DENSE_WEIGHTS)
    (sinks_ref, x1_ref, h_ref, u_ref, q_ref, kv_ref, hu_ref, hkv_ref,
     pw_ref, pb_ref, ps_ref, bg_ref, g2_ref) = refs[:13]
    dense_hbm = refs[13:13 + n_w]
    o_ref, uext, kx, vx, yb_scr, a_scr = refs[13 + n_w:19 + n_w]
    wg_ref, wpp_ref, wap_ref, wo_ref, w1_ref, w2_ref = dense_vmem = refs[19 + n_w:19 + 2 * n_w]
    lazy = _LazyWeights(dense_hbm, dense_vmem, refs[19 + 2 * n_w])
    lazy.start_all()

    seg = POOL_HIST_PAD + t_new
    for e in range(n_streams):
        uext[e * seg:e * seg + POOL_HIST_PAD, :] = hu_ref[e]
        uext[e * seg + POOL_HIST_PAD:(e + 1) * seg, :] = u_ref[e * t_new:(e + 1) * t_new, :]

    def take_new(a):
        return jnp.concatenate([a[e * seg + POOL_HIST_PAD:(e + 1) * seg] for e in range(n_streams)], axis=0)

    ya = _pool_mixer(uext, take_new, None, pw_ref, pb_ref, ps_ref)

    sink_cols = _sink_columns(sinks_ref, t_new)
    for e in range(n_streams):
        kx_e, vx_e = kx.at[e % 2], vx.at[e % 2]
        _fill_kv_variants(kx_e, vx_e, 0, hkv_ref[e])
        _fill_kv_variants(kx_e, vx_e, WINDOW, kv_ref[e * t_new:(e + 1) * t_new, :])
        _attend_block(q_ref, e * t_new, t_new, kx_e, vx_e, 0, WINDOW + t_new, None, sink_cols, yb_scr)

    o_ref[...] = _dense_tail(x1_ref[...], h_ref[...], ya, yb_scr[...], wg_ref, bg_ref, wpp_ref, wap_ref, wo_ref,
                             g2_ref, w1_ref, w2_ref, a_scr, lazy)
    assert lazy.finished()


def _const_spec(shape):
    return pl.BlockSpec(shape, lambda *_: (0,) * len(shape), pipeline_mode=pl.Buffered(1))


def _front_call(xp2d, xs2d, g1, w1, w2, gm, win, qn2, kn2, cast_srcs):
    n_p = xp2d.shape[0]
    tm = ROW_TILE
    assert n_p % tm == 0 and xs2d.shape[0] == tm
    n_prompt_steps = n_p // tm
    steps = n_prompt_steps + 1
    n = n_p + tm
    row = lambda width: pl.BlockSpec((tm, width), lambda i: (i, 0))
    cast_in, cast_out, cast_shapes = [], [], []
    for w, cols in cast_srcs:
        _, w_rows, w_cols = w.shape
        n_blk = max(d for d in range(1, steps + 1) if w_rows % (BF16_SUBLANES * d) == 0)
        rows = w_rows // n_blk
        assert cols % LANES == 0
        cast_in.append(pl.BlockSpec((None, rows, w_cols), lambda i, n_blk=n_blk: (0, jnp.minimum(i, n_blk - 1), 0)))
        cast_out.append(pl.BlockSpec((rows, cols), lambda i, n_blk=n_blk: (jnp.minimum(i, n_blk - 1), 0)))
        cast_shapes.append(jax.ShapeDtypeStruct((w_rows, cols), BF16))
    hbm = pl.BlockSpec(memory_space=pl.ANY)
    col_slots, row_slots = WEIGHT_STAGE_SLOTS
    return pl.pallas_call(
        functools.partial(_front_body, n_cast=len(cast_srcs), n_prompt_steps=n_prompt_steps),
        grid=(steps,),
        in_specs=[pl.BlockSpec((tm, D_MODEL), lambda i: (jnp.minimum(i, n_prompt_steps - 1), 0)),
                  _const_spec(xs2d.shape),
                  _const_spec(g1.shape), hbm, hbm, _const_spec(gm.shape), hbm,
                  _const_spec(qn2.shape), _const_spec(kn2.shape)]
                 + cast_in,
        out_specs=[row(D_MODEL), row(D_MODEL), row(D_POOL), row(D_Q), row(2 * D_KV)] + cast_out,
        out_shape=[jax.ShapeDtypeStruct((n, D_MODEL), F32), jax.ShapeDtypeStruct((n, D_MODEL), BF16),
                   jax.ShapeDtypeStruct((n, D_POOL), F32), jax.ShapeDtypeStruct((n, D_Q), BF16),
                   jax.ShapeDtypeStruct((n, 2 * D_KV), F32)] + cast_shapes,
        scratch_shapes=[pltpu.VMEM((tm, D_FF), BF16),
                        pltpu.VMEM((D_MODEL, 2 * D_FF), BF16), pltpu.VMEM((D_FF, D_MODEL), BF16),
                        pltpu.VMEM((D_MODEL, D_UQKV), BF16),
                        pltpu.VMEM((col_slots, D_MODEL, FF_CHUNK), F32), pltpu.VMEM((row_slots, FF_CHUNK, D_MODEL), F32),
                        pltpu.SemaphoreType.DMA((col_slots,)), pltpu.SemaphoreType.DMA((row_slots,))],
        compiler_params=pltpu.CompilerParams(dimension_semantics=("arbitrary",),
                                             vmem_limit_bytes=VMEM_LIMIT_BYTES),
        name="front",
    )(xp2d, xs2d, g1, w1, w2, gm, win, qn2, kn2, *[w for w, _ in cast_srcs])


def _dense_weight_scratch(dense_w):
    return ([pltpu.VMEM(w.shape, w.dtype) for w in dense_w] + [pltpu.SemaphoreType.DMA((len(dense_w),))])


def _prompt_mix_call(sinks, x1, h, u, q, kv, small_w, dense_w, n_seq, seq_len):
    tm = ROW_TILE
    assert seq_len % tm == 0 and len(small_w) == 5 and len(dense_w) == len(DENSE_WEIGHTS)
    tiles = seq_len // tm
    row = lambda width: pl.BlockSpec((tm, width), lambda b, t: (b * tiles + t, 0))
    return pl.pallas_call(
        _prompt_mix_body,
        grid=(n_seq, tiles),
        in_specs=[pl.BlockSpec(memory_space=pltpu.SMEM),
                  row(D_MODEL), row(D_MODEL), row(D_POOL), row(D_Q), row(2 * D_KV)]
                 + [_const_spec(w.shape) for w in tuple(small_w) + tuple(dense_w)],
        out_specs=row(D_MODEL),
        out_shape=jax.ShapeDtypeStruct((n_seq * seq_len, D_MODEL), F32),
        scratch_shapes=[pltpu.VMEM((POOL_HIST_PAD + tm, D_POOL), F32),
                        pltpu.VMEM((4, WINDOW + tm, LANES), BF16),
                        pltpu.VMEM((4, WINDOW + tm, 2 * LANES), BF16),
                        pltpu.VMEM((tm, D_Q), BF16),
                        pltpu.VMEM((tm, D_FF), BF16)],
        compiler_params=pltpu.CompilerParams(dimension_semantics=("arbitrary", "arbitrary"),
                                             vmem_limit_bytes=VMEM_LIMIT_BYTES),
        name="prompt_mix",
    )(sinks, x1, h, u, q, kv, *small_w, *dense_w)


def _sample_tail_call(sinks, x1, h, u, q, kv, hist_u, hist_kv, small_w, dense_w, row_block, n_streams, t_new):
    tm = ROW_TILE
    assert n_streams * t_new == tm and t_new % BF16_SUBLANES == 0 and len(dense_w) == len(DENSE_WEIGHTS)
    row = lambda width: pl.BlockSpec((tm, width), lambda i: (row_block, 0))
    return pl.pallas_call(
        functools.partial(_sample_tail_body, n_streams=n_streams, t_new=t_new),
        grid=(1,),
        in_specs=[pl.BlockSpec(memory_space=pltpu.SMEM),
                  row(D_MODEL), row(D_MODEL), row(D_POOL), row(D_Q), row(2 * D_KV),
                  _const_spec(hist_u.shape), _const_spec(hist_kv.shape)]
                 + [_const_spec(w.shape) for w in small_w]
                 + [pl.BlockSpec(memory_space=pl.ANY) for _ in dense_w],
        out_specs=pl.BlockSpec((tm, D_MODEL), lambda i: (0, 0)),
        out_shape=jax.ShapeDtypeStruct((tm, D_MODEL), F32),
        scratch_shapes=[pltpu.VMEM((n_streams * (POOL_HIST_PAD + t_new), D_POOL), F32),
                        pltpu.VMEM((2, 4, WINDOW + t_new, LANES), BF16),
                        pltpu.VMEM((2, 4, WINDOW + t_new, 2 * LANES), BF16),
                        pltpu.VMEM((tm, D_Q), BF16),
                        pltpu.VMEM((tm, D_FF), BF16)] + _dense_weight_scratch(dense_w),
        compiler_params=pltpu.CompilerParams(dimension_semantics=("arbitrary",),
                                             vmem_limit_bytes=VMEM_LIMIT_BYTES),
        name="sample_tail",
    )(sinks, x1, h, u, q, kv, hist_u, hist_kv, *small_w, *dense_w)


def kernel(x_prompt, x_sample, state_pool, cache_k, cache_v, norm_ffn1, ffn1_w_in, ffn1_w_out, norm_mix, w_in, b_gate, pool_w, pool_b, pool_scale, q_norm, k_norm, sinks, w_pool_proj, w_attn_proj, w_out, norm_ffn2, ffn2_w_in, ffn2_w_out):
    depth = norm_ffn1.shape[0]
    assert depth == 1, "single-layer trunk"
    bp, sp, _ = x_prompt.shape
    bs, ts, _ = x_sample.shape
    n_p = bp * sp

    front_w = (norm_ffn1[0].reshape(1, D_MODEL), ffn1_w_in, ffn1_w_out, norm_mix[0].reshape(1, D_MODEL), w_in,
               jnp.tile(q_norm[0], 2).reshape(1, LANES), jnp.tile(k_norm[0], 2).reshape(1, LANES))
    cast_srcs = ((w_in, 2 * D_MODEL), (w_pool_proj, D_MODEL), (w_attn_proj, D_MODEL), (w_out, D_MODEL),
                 (ffn2_w_in, 2 * D_FF), (ffn2_w_out, D_MODEL))
    x1, h, u, q, kv, wg, wpp, wap, wo, ffn2_w1, ffn2_w2 = _front_call(
        x_prompt.reshape(n_p, D_MODEL), x_sample.reshape(bs * ts, D_MODEL), *front_w, cast_srcs=cast_srcs)

    pw = pool_w[0].astype(BF16)
    zero_blk = jnp.zeros((POOL_GROUP, POOL_GROUP), BF16)
    pw_pairs = jnp.stack([jnp.block([[pw[2 * i], zero_blk], [zero_blk, pw[2 * i + 1]]])
                          for i in range(N_POOL_GROUPS // 2)])
    small_w = (pw_pairs, pool_b[0].reshape(1, D_POOL), pool_scale[0].reshape(1, D_POOL),
               b_gate[0].reshape(1, 2 * D_MODEL), norm_ffn2[0].reshape(1, D_MODEL))
    dense_w = (wg, wpp, wap, wo, ffn2_w1, ffn2_w2)
    sink_vec = sinks[0]

    y_prompt = _prompt_mix_call(sink_vec, x1, h, u, q, kv, small_w, dense_w, bp, sp).reshape(bp, sp, D_MODEL)
    new_pool_p = jnp.stack([u[(b + 1) * sp - POOL_HIST:(b + 1) * sp] for b in range(bp)])
    kv_last = jnp.stack([kv[(b + 1) * sp - WINDOW:(b + 1) * sp] for b in range(bp)])
    new_k_p = kv_last[..., :D_KV].reshape(bp, WINDOW, N_KV_HEADS, HEAD_DIM)
    new_v_p = kv_last[..., D_KV:].reshape(bp, WINDOW, N_KV_HEADS, HEAD_DIM)

    hist_u = jnp.pad(state_pool[0], ((0, 0), (POOL_HIST_PAD - POOL_HIST, 0), (0, 0)))
    hist_kv = jnp.concatenate([cache_k[0].reshape(bs, WINDOW, D_KV), cache_v[0].reshape(bs, WINDOW, D_KV)], axis=-1)
    y_sample = _sample_tail_call(sink_vec, x1, h, u, q, kv, hist_u, hist_kv, small_w, dense_w,
                                 n_p // ROW_TILE, bs, ts).reshape(bs, ts, D_MODEL)
    us3 = u[n_p:].reshape(bs, ts, D_POOL)
    kvs3 = kv[n_p:].reshape(bs, ts, 2 * D_KV)
    new_pool_s = jnp.concatenate([state_pool[0], us3], axis=1)[:, -POOL_HIST:]
    kv_full = jnp.concatenate([hist_kv, kvs3], axis=1)[:, -WINDOW:]
    new_k_s = kv_full[..., :D_KV].reshape(bs, WINDOW, N_KV_HEADS, HEAD_DIM)
    new_v_s = kv_full[..., D_KV:].reshape(bs, WINDOW, N_KV_HEADS, HEAD_DIM)

    return (y_prompt, y_sample, new_pool_p[None], new_k_p[None], new_v_p[None],
            new_pool_s[None], new_k_s[None], new_v_s[None])
```

```python
import functools

import jax
import jax.numpy as jnp
from jax import lax
from jax.experimental import pallas as pl
from jax.experimental.pallas import tpu as pltpu

D_MODEL = 1024
CHUNK = 64
N_HEADS = 16
N_KV_HEADS = 2
HEAD_DIM = 64
WINDOW = 128
D_POOL = 512
N_POOL_GROUPS = 4
POOL_GROUP = 128
POOL_WINDOWS = (2, 4, 8, 16)
POOL_HIST = 15
POOL_HIST_PAD = 16
D_Q = N_HEADS * HEAD_DIM
D_KV = N_KV_HEADS * HEAD_DIM
D_UQKV = D_POOL + D_Q + 2 * D_KV
D_FF = 2816
EPS = 1e-6

LANES = 128
BF16_SUBLANES = 16
FF_CHUNK = 256
N_FF_CHUNKS = D_FF // FF_CHUNK
PAIRS_PER_KV = (N_HEADS // N_KV_HEADS) // 2
ROW_TILE = 512
PROJ_BLOCK_ORDER = (2, 3, 4, 5, 6, 0, 1)
WEIGHT_STAGE_SLOTS = (5, 2)
VMEM_LIMIT_BYTES = 58 * 1024 * 1024
NEG_BIG = float(jnp.finfo(jnp.float32).min)
LOG2E = 1.4426950408889634

BF16 = jnp.bfloat16
F32 = jnp.float32


def _dot(a, b):
    return jnp.dot(a, b, preferred_element_type=F32)


def _dot_nt(a, b):
    return lax.dot_general(a, b, (((1,), (1,)), ((), ())), preferred_element_type=F32)


def _rms_scale(x):
    return lax.rsqrt(jnp.mean(x * x, axis=-1, keepdims=True) + EPS)


def _ffn(x, g_ref, w1_ref, w2_ref, a_scr, side_work=(), loader=None):
    xg = (x * g_ref[...]).astype(BF16)
    r = _rms_scale(x)
    for c in range(N_FF_CHUNKS):
        if loader is not None:
            loader.before_ffn_chunk(c)
        gate = _dot(xg, w1_ref[:, c * FF_CHUNK:(c + 1) * FF_CHUNK]) * r
        up = _dot(xg, w1_ref[:, D_FF + c * FF_CHUNK:D_FF + (c + 1) * FF_CHUNK]) * r
        a_scr[:, c * FF_CHUNK:(c + 1) * FF_CHUNK] = (gate * jax.nn.sigmoid(gate) * up).astype(BF16)
        if c < len(side_work):
            side_work[c]()
    assert len(side_work) <= N_FF_CHUNKS
    if loader is not None:
        loader.before_ffn_out()
    return x + 0.5 * _dot(a_scr[...], w2_ref[...])


def _half_sumsq(x):
    y = x * x
    low = lax.broadcasted_iota(jnp.int32, y.shape, 1) < HEAD_DIM
    sum_low = jnp.sum(jnp.where(low, y, 0.0), axis=-1, keepdims=True)
    sum_high = jnp.sum(jnp.where(low, 0.0, y), axis=-1, keepdims=True)
    return jnp.where(low, sum_low, sum_high)


def _head_norm(x, g2):
    return x * lax.rsqrt(_half_sumsq(x) * (1.0 / HEAD_DIM) + EPS) * g2


def _cast_block(src_ref, dst_ref):
    dst_ref[...] = src_ref[:, src_ref.shape[1] - dst_ref.shape[1]:].astype(BF16)


class _WeightLoader:
    def __init__(self, w1_hbm, w2_hbm, win_hbm, w1_bf, w2_bf, win_bf, stage_cols, stage_rows, sem_cols, sem_rows):
        blk = FF_CHUNK
        self._units = {"cols": [], "rows": []}
        self._stage = {"cols": stage_cols, "rows": stage_rows}
        self._sem = {"cols": sem_cols, "rows": sem_rows}
        self._done = {"cols": 0, "rows": 0}

        def col_unit(hbm, dst, col):
            return (hbm.at[0, :, pl.ds(col, blk)], dst, (slice(None), slice(col, col + blk)))

        for c in range(N_FF_CHUNKS):
            self._units["cols"].append(col_unit(w1_hbm, w1_bf, c * blk))
            self._units["cols"].append(col_unit(w1_hbm, w1_bf, D_FF + c * blk))
            self._units["rows"].append((w2_hbm.at[0, pl.ds(c * blk, blk), :], w2_bf,
                                        (slice(c * blk, (c + 1) * blk), slice(None))))
        self._n_ffn_cols = len(self._units["cols"])
        for b in PROJ_BLOCK_ORDER:
            self._units["cols"].append(col_unit(win_hbm, win_bf, b * blk))

    def _copy(self, kind, k):
        n_slots = self._stage[kind].shape[0]
        src, _, _ = self._units[kind][k]
        return pltpu.make_async_copy(src, self._stage[kind].at[k % n_slots], self._sem[kind].at[k % n_slots])

    def start_all(self):
        for kind in ("cols", "rows"):
            for k in range(min(self._stage[kind].shape[0], len(self._units[kind]))):
                self._copy(kind, k).start()

    def _land(self, kind, upto):
        n_slots = self._stage[kind].shape[0]
        for k in range(self._done[kind], upto):
            self._copy(kind, k).wait()
            _, dst, idx = self._units[kind][k]
            dst[idx] = self._stage[kind][k % n_slots].astype(BF16)
            if k + n_slots < len(self._units[kind]):
                self._copy(kind, k + n_slots).start()
        self._done[kind] = max(self._done[kind], upto)

    def before_ffn_chunk(self, c):
        self._land("cols", 2 * (c + 1))
        self._land("rows", c)

    def before_ffn_out(self):
        self._land("rows", N_FF_CHUNKS)

    def before_proj_blocks(self, n_used):
        self._land("cols", self._n_ffn_cols + n_used)

    def land_all(self):
        for c in range(N_FF_CHUNKS):
            self.before_ffn_chunk(c)
        self.before_ffn_out()
        self.before_proj_blocks(len(PROJ_BLOCK_ORDER))

    def finished(self):
        return all(self._done[kind] == len(self._units[kind]) for kind in self._units)


def _front_compute(x, g1_ref, w1_bf, w2_bf, gm_ref, win_bf, qn_ref, kn_ref,
                   x1_ref, h_ref, u_ref, q_ref, kv_ref, a_scr, casts, loader):
    x1 = _ffn(x, g1_ref, w1_bf, w2_bf, a_scr, casts, loader)
    x1_ref[...] = x1
    xg = x1 * gm_ref[...]
    r = _rms_scale(x1)
    h_ref[...] = (xg * r).astype(BF16)
    h = xg.astype(BF16)
    qg = qn_ref[...] * (HEAD_DIM ** -0.5 * LOG2E)
    blk = 2 * LANES
    for b in range(D_Q // blk):
        if loader is not None:
            loader.before_proj_blocks(b + 1)
        zq = _dot(h, win_bf[:, D_POOL + b * blk:D_POOL + (b + 1) * blk]) * r
        for p in range(2):
            q_ref[:, b * blk + p * LANES:b * blk + (p + 1) * LANES] = _head_norm(
                zq[:, p * LANES:(p + 1) * LANES], qg).astype(BF16)
    if loader is not None:
        loader.before_proj_blocks(D_Q // blk + 1)
    zkv = _dot(h, win_bf[:, D_POOL + D_Q:D_POOL + D_Q + 2 * D_KV]) * r
    kv_ref[:, :D_KV] = _head_norm(zkv[:, :D_KV], kn_ref[...])
    kv_ref[:, D_KV:] = zkv[:, D_KV:]
    if loader is not None:
        loader.before_proj_blocks(len(PROJ_BLOCK_ORDER))
    u_ref[...] = _dot(h, win_bf[:, :D_POOL]) * r


def _front_body(*refs, n_cast, n_prompt_steps):
    xp_ref, xs_ref, g1_ref, w1_hbm, w2_hbm, gm_ref, win_hbm, qn_ref, kn_ref = refs[:9]
    cast_src = refs[9:9 + n_cast]
    outs = refs[9 + n_cast:14 + n_cast]
    cast_dst = refs[14 + n_cast:14 + 2 * n_cast]
    a_scr, w1_bf, w2_bf, win_bf, stage_cols, stage_rows, sem_cols, sem_rows = refs[14 + 2 * n_cast:]
    casts = [functools.partial(_cast_block, src, dst) for src, dst in zip(cast_src, cast_dst)]
    i = pl.program_id(0)

    @pl.when(i == 0)
    def _():
        loader = _WeightLoader(w1_hbm, w2_hbm, win_hbm, w1_bf, w2_bf, win_bf,
                               stage_cols, stage_rows, sem_cols, sem_rows)
        loader.start_all()
        loader.land_all()
        assert loader.finished()

    @pl.when(i > 0)
    def _():
        x = jnp.where(i <= n_prompt_steps, xp_ref[...], xs_ref[...])
        _front_compute(x, g1_ref, w1_bf, w2_bf, gm_ref, win_bf, qn_ref, kn_ref, *outs, a_scr, casts, None)


def _pool_pair(ext_ref, take_new, first_row, pw_ref, pb_ref, ps_ref, pair):
    pooled = []
    for gi in (2 * pair, 2 * pair + 1):
        w = POOL_WINDOWS[gi]
        ext = ext_ref[:, gi * POOL_GROUP:(gi + 1) * POOL_GROUP]
        u_new = take_new(ext)
        s = ext
        k = 1
        while k < w:
            s = s + pltpu.roll(s, k, axis=0)
            k *= 2
        s = take_new(s)
        if first_row is None:
            mean = s * (1.0 / w)
        else:
            pos = first_row + lax.broadcasted_iota(jnp.int32, (s.shape[0], 1), 0)
            mean = s * (1.0 / jnp.minimum(pos + 1, w).astype(F32))
        pooled.append((mean - u_new).astype(BF16))
    cols = slice(2 * pair * POOL_GROUP, 2 * (pair + 1) * POOL_GROUP)
    return (_dot(jnp.concatenate(pooled, axis=-1), pw_ref[pair]) + pb_ref[:, cols]) * ps_ref[:, cols]


def _pool_mixer(ext_ref, take_new, first_row, pw_ref, pb_ref, ps_ref):
    return jnp.concatenate([_pool_pair(ext_ref, take_new, first_row, pw_ref, pb_ref, ps_ref, i).astype(BF16)
                            for i in range(N_POOL_GROUPS // 2)], axis=-1)


def _fill_kv_variants(kx_ref, vx_ref, row0, kv):
    rows = kv.shape[0]
    low = lax.broadcasted_iota(jnp.int32, (rows, LANES), 1) < HEAD_DIM
    ones = (jnp.where(low, 1.0, 0.0).astype(BF16), jnp.where(low, 0.0, 1.0).astype(BF16))
    for t, base in ((0, 0), (1, D_KV)):
        both = kv[:, base:base + D_KV]
        swapped = pltpu.roll(both, HEAD_DIM, axis=1)
        variants = (jnp.where(low, both, 0.0), jnp.where(low, 0.0, swapped),
                    jnp.where(low, swapped, 0.0), jnp.where(low, 0.0, both))
        for i, val in enumerate(variants):
            if t == 0:
                kx_ref[i, row0:row0 + rows, :] = val.astype(BF16)
            else:
                vx_ref[i, row0:row0 + rows, :LANES] = val.astype(BF16)
                vx_ref[i, row0:row0 + rows, LANES:] = ones[i % 2]


def _sink_columns(sinks_ref, q_rows):
    m_rows = PAIRS_PER_KV * q_rows
    row_blk = lax.broadcasted_iota(jnp.int32, (m_rows, 1), 0) // q_rows
    cols = []
    for kh in range(N_KV_HEADS):
        pair = []
        for ab in range(2):
            sink = jnp.zeros((m_rows, 1), F32)
            for pp in range(PAIRS_PER_KV):
                sink = jnp.where(row_blk == pp, sinks_ref[kh * 2 * PAIRS_PER_KV + 2 * pp + ab] * LOG2E, sink)
            pair.append(sink)
        cols.append(pair)
    return cols


def _attend_block(q_ref, q_row0, q_rows, kx_ref, vx_ref, key_row0, n_keys, key_valid, sink_cols, yb_ref):
    m_rows = PAIRS_PER_KV * q_rows
    low = lax.broadcasted_iota(jnp.int32, (m_rows, LANES), 1) < HEAD_DIM
    for kh in range(N_KV_HEADS):
        qs = jnp.concatenate(
            [q_ref[q_row0:q_row0 + q_rows, (kh * PAIRS_PER_KV + pp) * LANES:(kh * PAIRS_PER_KV + pp + 1) * LANES]
             for pp in range(PAIRS_PER_KV)], axis=0)
        o = None
        row_max = []
        for ab in range(2):
            s = _dot_nt(qs, kx_ref[2 * kh + ab, key_row0:key_row0 + n_keys, :])
            if key_valid is not None:
                s = jnp.where(key_valid, s, NEG_BIG)
            m = jnp.maximum(jnp.max(s, axis=-1, keepdims=True), sink_cols[kh][ab])
            p = jnp.exp2(s - m).astype(BF16)
            pv = _dot(p, vx_ref[2 * kh + ab, key_row0:key_row0 + n_keys, :])
            o = pv if o is None else o + pv
            row_max.append(m)
        sink_p = jnp.exp2(jnp.where(low, sink_cols[kh][0], sink_cols[kh][1])
                          - jnp.where(low, row_max[0], row_max[1]))
        out = o[:, :LANES] / (o[:, LANES:] + sink_p)
        for pp in range(PAIRS_PER_KV):
            col = (kh * PAIRS_PER_KV + pp) * LANES
            yb_ref[q_row0:q_row0 + q_rows, col:col + LANES] = out[pp * q_rows:(pp + 1) * q_rows].astype(BF16)


DENSE_WEIGHTS = ("wg", "wpp", "wap", "wo", "w1", "w2")


class _LazyWeights:
    def __init__(self, hbm_refs, vmem_refs, sem):
        assert len(hbm_refs) == len(vmem_refs) == len(DENSE_WEIGHTS)
        self._copies = [pltpu.make_async_copy(src, dst, sem.at[k])
                        for k, (src, dst) in enumerate(zip(hbm_refs, vmem_refs))]
        self._waited = set()

    def start_all(self):
        for copy in self._copies:
            copy.start()

    def wait(self, name):
        k = DENSE_WEIGHTS.index(name)
        if k not in self._waited:
            self._copies[k].wait()
            self._waited.add(k)

    def before_ffn_chunk(self, c):
        if c == 0:
            self.wait("w1")

    def before_ffn_out(self):
        self.wait("w2")

    def finished(self):
        return len(self._waited) == len(self._copies)


def _dense_tail(x1, h, ya, yb, wg_ref, bg_ref, wpp_ref, wap_ref, wo_ref, g2_ref, w1_ref, w2_ref, a_scr, lazy=None):
    wait = lazy.wait if lazy is not None else (lambda name: None)
    wait("wg")
    gates = jax.nn.sigmoid(_dot(h, wg_ref[...]) + bg_ref[...])
    wait("wpp")
    a = _dot(ya, wpp_ref[...])
    wait("wap")
    b = _dot(yb, wap_ref[...])
    mixed = gates[:, :D_MODEL] * a + gates[:, D_MODEL:] * b
    wait("wo")
    x2 = x1 + _dot(mixed.astype(BF16), wo_ref[...])
    return _ffn(x2, g2_ref, w1_ref, w2_ref, a_scr, loader=lazy)


def _prompt_mix_tile(sinks_ref, x1_ref, h_ref, u_ref, q_ref, kv_ref, pw_ref, pb_ref, ps_ref, bg_ref, g2_ref,
                     dense_w, o_ref, uext, kx, vx, yb_scr, a_scr, lazy):
    rows = x1_ref.shape[0]
    t = pl.program_id(1)
    wg_ref, wpp_ref, wap_ref, wo_ref, w1_ref, w2_ref = dense_w

    @pl.when(t == 0)
    def _():
        uext[:POOL_HIST_PAD, :] = jnp.zeros((POOL_HIST_PAD, D_POOL), F32)
        kx[:, :WINDOW, :] = jnp.zeros((4, WINDOW, LANES), BF16)
        vx[:, :WINDOW, :] = jnp.zeros((4, WINDOW, 2 * LANES), BF16)

    uext[POOL_HIST_PAD:, :] = u_ref[...]
    _fill_kv_variants(kx, vx, WINDOW, kv_ref[...])

    ya = _pool_mixer(uext, lambda a: a[POOL_HIST_PAD:], t * rows, pw_ref, pb_ref, ps_ref)

    n_chunks = rows // CHUNK
    n_keys = WINDOW + CHUNK
    key_chunk = lax.broadcasted_iota(jnp.int32, (1, n_keys), 1) // CHUNK
    sink_cols = _sink_columns(sinks_ref, CHUNK)
    for j in range(n_chunks):
        first_chunk = t * n_chunks + j - WINDOW // CHUNK
        key_valid = key_chunk + first_chunk >= 0 if j < WINDOW // CHUNK else None
        _attend_block(q_ref, j * CHUNK, CHUNK, kx, vx, j * CHUNK, n_keys, key_valid, sink_cols, yb_scr)

    o_ref[...] = _dense_tail(x1_ref[...], h_ref[...], ya, yb_scr[...], wg_ref, bg_ref, wpp_ref, wap_ref, wo_ref,
                             g2_ref, w1_ref, w2_ref, a_scr, lazy)

    uext[:POOL_HIST_PAD, :] = uext[rows:rows + POOL_HIST_PAD, :]
    kx[:, :WINDOW, :] = kx[:, rows:rows + WINDOW, :]
    vx[:, :WINDOW, :] = vx[:, rows:rows + WINDOW, :]


def _prompt_mix_body(*refs):
    n_small = 11
    n_w = len(DENSE_WEIGHTS)
    _prompt_mix_tile(*refs[:n_small], refs[n_small:n_small + n_w], *refs[n_small + n_w:], None)


def _sample_tail_body(*refs, n_streams, t_new):
    n_w = len(DENSE_WEIGHTS)
    (sinks_ref, x1_ref, h_ref, u_ref, q_ref, kv_ref, hu_ref, hkv_ref,
     pw_ref, pb_ref, ps_ref, bg_ref, g2_ref) = refs[:13]
    dense_hbm = refs[13:13 + n_w]
    o_ref, uext, kx, vx, yb_scr, a_scr = refs[13 + n_w:19 + n_w]
    wg_ref, wpp_ref, wap_ref, wo_ref, w1_ref, w2_ref = dense_vmem = refs[19 + n_w:19 + 2 * n_w]
    lazy = _LazyWeights(dense_hbm, dense_vmem, refs[19 + 2 * n_w])
    lazy.start_all()

    seg = POOL_HIST_PAD + t_new
    for e in range(n_streams):
        uext[e * seg:e * seg + POOL_HIST_PAD, :] = hu_ref[e]
        uext[e * seg + POOL_HIST_PAD:(e + 1) * seg, :] = u_ref[e * t_new:(e + 1) * t_new, :]

    def take_new(a):
        return jnp.concatenate([a[e * seg + POOL_HIST_PAD:(e + 1) * seg] for e in range(n_streams)], axis=0)

    ya = _pool_mixer(uext, take_new, None, pw_ref, pb_ref, ps_ref)

    sink_cols = _sink_columns(sinks_ref, t_new)
    for e in range(n_streams):
        kx_e, vx_e = kx.at[e % 2], vx.at[e % 2]
        _fill_kv_variants(kx_e, vx_e, 0, hkv_ref[e])
        _fill_kv_variants(kx_e, vx_e, WINDOW, kv_ref[e * t_new:(e + 1) * t_new, :])
        _attend_block(q_ref, e * t_new, t_new, kx_e, vx_e, 0, WINDOW + t_new, None, sink_cols, yb_scr)

    o_ref[...] = _dense_tail(x1_ref[...], h_ref[...], ya, yb_scr[...], wg_ref, bg_ref, wpp_ref, wap_ref, wo_ref,
                             g2_ref, w1_ref, w2_ref, a_scr, lazy)
    assert lazy.finished()


def _const_spec(shape):
    return pl.BlockSpec(shape, lambda *_: (0,) * len(shape), pipeline_mode=pl.Buffered(1))


def _front_call(xp2d, xs2d, g1, w1, w2, gm, win, qn2, kn2, cast_srcs):
    n_p = xp2d.shape[0]
    tm = ROW_TILE
    assert n_p % tm == 0 and xs2d.shape[0] == tm
    n_prompt_steps = n_p // tm
    tiles = n_prompt_steps + 1
    n = n_p + tm
    tile_of = lambda i: jnp.maximum(i - 1, 0)
    row = lambda width: pl.BlockSpec((tm, width), lambda i: (tile_of(i), 0))
    cast_in, cast_out, cast_shapes = [], [], []
    for w, cols in cast_srcs:
        _, w_rows, w_cols = w.shape
        n_blk = max(d for d in range(1, tiles + 1) if w_rows % (BF16_SUBLANES * d) == 0)
        rows = w_rows // n_blk
        assert cols % LANES == 0
        cast_in.append(pl.BlockSpec((None, rows, w_cols), lambda i, n_blk=n_blk: (0, jnp.minimum(tile_of(i), n_blk - 1), 0)))
        cast_out.append(pl.BlockSpec((rows, cols), lambda i, n_blk=n_blk: (jnp.minimum(tile_of(i), n_blk - 1), 0)))
        cast_shapes.append(jax.ShapeDtypeStruct((w_rows, cols), BF16))
    hbm = pl.BlockSpec(memory_space=pl.ANY)
    col_slots, row_slots = WEIGHT_STAGE_SLOTS
    return pl.pallas_call(
        functools.partial(_front_body, n_cast=len(cast_srcs), n_prompt_steps=n_prompt_steps),
        grid=(tiles + 1,),
        in_specs=[pl.BlockSpec((tm, D_MODEL), lambda i: (jnp.minimum(tile_of(i), n_prompt_steps - 1), 0)),
                  _const_spec(xs2d.shape),
                  _const_spec(g1.shape), hbm, hbm, _const_spec(gm.shape), hbm,
                  _const_spec(qn2.shape), _const_spec(kn2.shape)]
                 + cast_in,
        out_specs=[row(D_MODEL), row(D_MODEL), row(D_POOL), row(D_Q), row(2 * D_KV)] + cast_out,
        out_shape=[jax.ShapeDtypeStruct((n, D_MODEL), F32), jax.ShapeDtypeStruct((n, D_MODEL), BF16),
                   jax.ShapeDtypeStruct((n, D_POOL), F32), jax.ShapeDtypeStruct((n, D_Q), BF16),
                   jax.ShapeDtypeStruct((n, 2 * D_KV), F32)] + cast_shapes,
        scratch_shapes=[pltpu.VMEM((tm, D_FF), BF16),
                        pltpu.VMEM((D_MODEL, 2 * D_FF), BF16), pltpu.VMEM((D_FF, D_MODEL), BF16),
                        pltpu.VMEM((D_MODEL, D_UQKV), BF16),
                        pltpu.VMEM((col_slots, D_MODEL, FF_CHUNK), F32), pltpu.VMEM((row_slots, FF_CHUNK, D_MODEL), F32),
                        pltpu.SemaphoreType.DMA((col_slots,)), pltpu.SemaphoreType.DMA((row_slots,))],
        compiler_params=pltpu.CompilerParams(dimension_semantics=("arbitrary",),
                                             vmem_limit_bytes=VMEM_LIMIT_BYTES),
        name="front",
    )(xp2d, xs2d, g1, w1, w2, gm, win, qn2, kn2, *[w for w, _ in cast_srcs])


def _dense_weight_scratch(dense_w):
    return ([pltpu.VMEM(w.shape, w.dtype) for w in dense_w] + [pltpu.SemaphoreType.DMA((len(dense_w),))])


def _prompt_mix_call(sinks, x1, h, u, q, kv, small_w, dense_w, n_seq, seq_len):
    tm = ROW_TILE
    assert seq_len % tm == 0 and len(small_w) == 5 and len(dense_w) == len(DENSE_WEIGHTS)
    tiles = seq_len // tm
    row = lambda width: pl.BlockSpec((tm, width), lambda b, t: (b * tiles + t, 0))
    return pl.pallas_call(
        _prompt_mix_body,
        grid=(n_seq, tiles),
        in_specs=[pl.BlockSpec(memory_space=pltpu.SMEM),
                  row(D_MODEL), row(D_MODEL), row(D_POOL), row(D_Q), row(2 * D_KV)]
                 + [_const_spec(w.shape) for w in tuple(small_w) + tuple(dense_w)],
        out_specs=row(D_MODEL),
        out_shape=jax.ShapeDtypeStruct((n_seq * seq_len, D_MODEL), F32),
        scratch_shapes=[pltpu.VMEM((POOL_HIST_PAD + tm, D_POOL), F32),
                        pltpu.VMEM((4, WINDOW + tm, LANES), BF16),
                        pltpu.VMEM((4, WINDOW + tm, 2 * LANES), BF16),
                        pltpu.VMEM((tm, D_Q), BF16),
                        pltpu.VMEM((tm, D_FF), BF16)],
        compiler_params=pltpu.CompilerParams(dimension_semantics=("arbitrary", "arbitrary"),
                                             vmem_limit_bytes=VMEM_LIMIT_BYTES),
        name="prompt_mix",
    )(sinks, x1, h, u, q, kv, *small_w, *dense_w)


def _sample_tail_call(sinks, x1, h, u, q, kv, hist_u, hist_kv, small_w, dense_w, row_block, n_streams, t_new):
    tm = ROW_TILE
    assert n_streams * t_new == tm and t_new % BF16_SUBLANES == 0 and len(dense_w) == len(DENSE_WEIGHTS)
    row = lambda width: pl.BlockSpec((tm, width), lambda i: (row_block, 0))
    return pl.pallas_call(
        functools.partial(_sample_tail_body, n_streams=n_streams, t_new=t_new),
        grid=(1,),
        in_specs=[pl.BlockSpec(memory_space=pltpu.SMEM),
                  row(D_MODEL), row(D_MODEL), row(D_POOL), row(D_Q), row(2 * D_KV),
                  _const_spec(hist_u.shape), _const_spec(hist_kv.shape)]
                 + [_const_spec(w.shape) for w in small_w]
                 + [pl.BlockSpec(memory_space=pl.ANY) for _ in dense_w],
        out_specs=pl.BlockSpec((tm, D_MODEL), lambda i: (0, 0)),
        out_shape=jax.ShapeDtypeStruct((tm, D_MODEL), F32),
        scratch_shapes=[pltpu.VMEM((n_streams * (POOL_HIST_PAD + t_new), D_POOL), F32),
                        pltpu.VMEM((2, 4, WINDOW + t_new, LANES), BF16),
                        pltpu.VMEM((2, 4, WINDOW + t_new, 2 * LANES), BF16),
                        pltpu.VMEM((tm, D_Q), BF16),
                        pltpu.VMEM((tm, D_FF), BF16)] + _dense_weight_scratch(dense_w),
        compiler_params=pltpu.CompilerParams(dimension_semantics=("arbitrary",),
                                             vmem_limit_bytes=VMEM_LIMIT_BYTES),
        name="sample_tail",
    )(sinks, x1, h, u, q, kv, hist_u, hist_kv, *small_w, *dense_w)


def kernel(x_prompt, x_sample, state_pool, cache_k, cache_v, norm_ffn1, ffn1_w_in, ffn1_w_out, norm_mix, w_in, b_gate, pool_w, pool_b, pool_scale, q_norm, k_norm, sinks, w_pool_proj, w_attn_proj, w_out, norm_ffn2, ffn2_w_in, ffn2_w_out):
    depth = norm_ffn1.shape[0]
    assert depth == 1, "single-layer trunk"
    bp, sp, _ = x_prompt.shape
    bs, ts, _ = x_sample.shape
    n_p = bp * sp

    front_w = (norm_ffn1[0].reshape(1, D_MODEL), ffn1_w_in, ffn1_w_out, norm_mix[0].reshape(1, D_MODEL), w_in,
               jnp.tile(q_norm[0], 2).reshape(1, LANES), jnp.tile(k_norm[0], 2).reshape(1, LANES))
    cast_srcs = ((w_in, 2 * D_MODEL), (w_pool_proj, D_MODEL), (w_attn_proj, D_MODEL), (w_out, D_MODEL),
                 (ffn2_w_in, 2 * D_FF), (ffn2_w_out, D_MODEL))
    x1, h, u, q, kv, wg, wpp, wap, wo, ffn2_w1, ffn2_w2 = _front_call(
        x_prompt.reshape(n_p, D_MODEL), x_sample.reshape(bs * ts, D_MODEL), *front_w, cast_srcs=cast_srcs)

    pw = pool_w[0].astype(BF16)
    zero_blk = jnp.zeros((POOL_GROUP, POOL_GROUP), BF16)
    pw_pairs = jnp.stack([jnp.block([[pw[2 * i], zero_blk], [zero_blk, pw[2 * i + 1]]])
                          for i in range(N_POOL_GROUPS // 2)])
    small_w = (pw_pairs, pool_b[0].reshape(1, D_POOL), pool_scale[0].reshape(1, D_POOL),
               b_gate[0].reshape(1, 2 * D_MODEL), norm_ffn2[0].reshape(1, D_MODEL))
    dense_w = (wg, wpp, wap, wo, ffn2_w1, ffn2_w2)
    sink_vec = sinks[0]

    y_prompt = _prompt_mix_call(sink_vec, x1, h, u, q, kv, small_w, dense_w, bp, sp).reshape(bp, sp, D_MODEL)
    new_pool_p = jnp.stack([u[(b + 1) * sp - POOL_HIST:(b + 1) * sp] for b in range(bp)])
    kv_last = jnp.stack([kv[(b + 1) * sp - WINDOW:(b + 1) * sp] for b in range(bp)])
    new_k_p = kv_last[..., :D_KV].reshape(bp, WINDOW, N_KV_HEADS, HEAD_DIM)
    new_v_p = kv_last[..., D_KV:].reshape(bp, WINDOW, N_KV_HEADS, HEAD_DIM)

    hist_u = jnp.pad(state_pool[0], ((0, 0), (POOL_HIST_PAD - POOL_HIST, 0), (0, 0)))
    hist_kv = jnp.concatenate([cache_k[0].reshape(bs, WINDOW, D_KV), cache_v[0].reshape(bs, WINDOW, D_KV)], axis=-1)
    y_sample = _sample_tail_call(sink_vec, x1, h, u, q, kv, hist_u, hist_kv, small_w, dense_w,
                                 n_p // ROW_TILE, bs, ts).reshape(bs, ts, D_MODEL)
    us3 = u[n_p:].reshape(bs, ts, D_POOL)
    kvs3 = kv[n_p:].reshape(bs, ts, 2 * D_KV)
    new_pool_s = jnp.concatenate([state_pool[0], us3], axis=1)[:, -POOL_HIST:]
    kv_full = jnp.concatenate([hist_kv, kvs3], axis=1)[:, -WINDOW:]
    new_k_s = kv_full[..., :D_KV].reshape(bs, WINDOW, N_KV_HEADS, HEAD_DIM)
    new_v_s = kv_full[..., D_KV:].reshape(bs, WINDOW, N_KV_HEADS, HEAD_DIM)

    return (y_prompt, y_sample, new_pool_p[None], new_k_p[None], new_v_p[None],
            new_pool_s[None], new_k_s[None], new_v_s[None])
```

```python
import functools

import jax
import jax.numpy as jnp
from jax import lax
from jax.experimental import pallas as pl
from jax.experimental.pallas import tpu as pltpu

D_MODEL = 1024
CHUNK = 64
N_HEADS = 16
N_KV_HEADS = 2
HEAD_DIM = 64
WINDOW = 128
D_POOL = 512
N_POOL_GROUPS = 4
POOL_GROUP = 128
POOL_WINDOWS = (2, 4, 8, 16)
POOL_HIST = 15
POOL_HIST_PAD = 16
D_Q = N_HEADS * HEAD_DIM
D_KV = N_KV_HEADS * HEAD_DIM
D_UQKV = D_POOL + D_Q + 2 * D_KV
D_FF = 2816
EPS = 1e-6

LANES = 128
BF16_SUBLANES = 16
FF_CHUNK = 256
N_FF_CHUNKS = D_FF // FF_CHUNK
PAIRS_PER_KV = (N_HEADS // N_KV_HEADS) // 2
ROW_TILE = 512
WEIGHT_STAGE_SLOTS = 2
WEIGHT_STAGE_ROWS = (64, 256, 64)
VMEM_LIMIT_BYTES = 58 * 1024 * 1024
NEG_BIG = float(jnp.finfo(jnp.float32).min)
LOG2E = 1.4426950408889634

BF16 = jnp.bfloat16
F32 = jnp.float32


def _dot(a, b):
    return jnp.dot(a, b, preferred_element_type=F32)


def _dot_nt(a, b):
    return lax.dot_general(a, b, (((1,), (1,)), ((), ())), preferred_element_type=F32)


def _rms_scale(x):
    return lax.rsqrt(jnp.mean(x * x, axis=-1, keepdims=True) + EPS)


def _ffn(x, g_ref, w1_ref, w2_ref, a_scr, side_work=(), loader=None):
    xg = (x * g_ref[...]).astype(BF16)
    r = _rms_scale(x)
    for c in range(N_FF_CHUNKS):
        if loader is not None:
            loader.before_ffn_chunk(c)
        gate = _dot(xg, w1_ref[:, c * FF_CHUNK:(c + 1) * FF_CHUNK]) * r
        up = _dot(xg, w1_ref[:, D_FF + c * FF_CHUNK:D_FF + (c + 1) * FF_CHUNK]) * r
        a_scr[:, c * FF_CHUNK:(c + 1) * FF_CHUNK] = (gate * jax.nn.sigmoid(gate) * up).astype(BF16)
        if c < len(side_work):
            side_work[c]()
    assert len(side_work) <= N_FF_CHUNKS
    if loader is not None:
        loader.before_ffn_out()
    return x + 0.5 * _dot(a_scr[...], w2_ref[...])


def _half_sumsq(x):
    y = x * x
    low = lax.broadcasted_iota(jnp.int32, y.shape, 1) < HEAD_DIM
    sum_low = jnp.sum(jnp.where(low, y, 0.0), axis=-1, keepdims=True)
    sum_high = jnp.sum(jnp.where(low, 0.0, y), axis=-1, keepdims=True)
    return jnp.where(low, sum_low, sum_high)


def _head_norm(x, g2):
    return x * lax.rsqrt(_half_sumsq(x) * (1.0 / HEAD_DIM) + EPS) * g2


def _cast_block(src_ref, dst_ref):
    dst_ref[...] = src_ref[:, src_ref.shape[1] - dst_ref.shape[1]:].astype(BF16)


class _WeightLoader:
    def __init__(self, streams):
        self._streams = streams

    def _copy(self, s, k):
        hbm, _, stage, sem = self._streams[s]
        slots, rows, _ = stage.shape
        return pltpu.make_async_copy(hbm.at[0, pl.ds(k * rows, rows), :], stage.at[k % slots], sem.at[k % slots])

    def run(self):
        n_blocks = []
        for s, (hbm, _, stage, _) in enumerate(self._streams):
            slots, rows, _ = stage.shape
            assert hbm.shape[1] % rows == 0
            n_blocks.append(hbm.shape[1] // rows)
            for k in range(min(slots, n_blocks[s])):
                self._copy(s, k).start()
        for k in range(max(n_blocks)):
            for s, (_, dst, stage, _) in enumerate(self._streams):
                if k >= n_blocks[s]:
                    continue
                slots, rows, _ = stage.shape
                self._copy(s, k).wait()
                dst[k * rows:(k + 1) * rows, :] = stage[k % slots][:, :dst.shape[1]].astype(BF16)
                if k + slots < n_blocks[s]:
                    self._copy(s, k + slots).start()


def _front_compute(x, g1_ref, w1_bf, w2_bf, gm_ref, win_bf, qn_ref, kn_ref,
                   x1_ref, h_ref, u_ref, q_ref, kv_ref, a_scr, casts):
    x1 = _ffn(x, g1_ref, w1_bf, w2_bf, a_scr, casts)
    x1_ref[...] = x1
    xg = x1 * gm_ref[...]
    r = _rms_scale(x1)
    h_ref[...] = (xg * r).astype(BF16)
    h = xg.astype(BF16)
    qg = qn_ref[...] * (HEAD_DIM ** -0.5 * LOG2E)
    blk = 2 * LANES
    for b in range(D_Q // blk):
        zq = _dot(h, win_bf[:, D_POOL + b * blk:D_POOL + (b + 1) * blk]) * r
        for p in range(2):
            q_ref[:, b * blk + p * LANES:b * blk + (p + 1) * LANES] = _head_norm(
                zq[:, p * LANES:(p + 1) * LANES], qg).astype(BF16)
    zkv = _dot(h, win_bf[:, D_POOL + D_Q:D_POOL + D_Q + 2 * D_KV]) * r
    kv_ref[:, :D_KV] = _head_norm(zkv[:, :D_KV], kn_ref[...])
    kv_ref[:, D_KV:] = zkv[:, D_KV:]
    u_ref[...] = _dot(h, win_bf[:, :D_POOL]) * r


def _front_body(*refs, n_cast, n_prompt_steps):
    xp_ref, xs_ref, g1_ref, w1_hbm, w2_hbm, gm_ref, win_hbm, qn_ref, kn_ref = refs[:9]
    cast_src = refs[9:9 + n_cast]
    outs = refs[9 + n_cast:14 + n_cast]
    cast_dst = refs[14 + n_cast:14 + 2 * n_cast]
    a_scr, w1_bf, w2_bf, win_bf = refs[14 + 2 * n_cast:18 + 2 * n_cast]
    stages = refs[18 + 2 * n_cast:21 + 2 * n_cast]
    sems = refs[21 + 2 * n_cast:24 + 2 * n_cast]
    casts = [functools.partial(_cast_block, src, dst) for src, dst in zip(cast_src, cast_dst)]
    i = pl.program_id(0)

    @pl.when(i == 0)
    def _():
        _WeightLoader(list(zip((w1_hbm, w2_hbm, win_hbm), (w1_bf, w2_bf, win_bf), stages, sems))).run()

    @pl.when(i > 0)
    def _():
        x = jnp.where(i <= n_prompt_steps, xp_ref[...], xs_ref[...])
        _front_compute(x, g1_ref, w1_bf, w2_bf, gm_ref, win_bf, qn_ref, kn_ref, *outs, a_scr, casts)


def _pool_pair(ext_ref, take_new, first_row, pw_ref, pb_ref, ps_ref, pair):
    pooled = []
    for gi in (2 * pair, 2 * pair + 1):
        w = POOL_WINDOWS[gi]
        ext = ext_ref[:, gi * POOL_GROUP:(gi + 1) * POOL_GROUP]
        u_new = take_new(ext)
        s = ext
        k = 1
        while k < w:
            s = s + pltpu.roll(s, k, axis=0)
            k *= 2
        s = take_new(s)
        if first_row is None:
            mean = s * (1.0 / w)
        else:
            pos = first_row + lax.broadcasted_iota(jnp.int32, (s.shape[0], 1), 0)
            mean = s * (1.0 / jnp.minimum(pos + 1, w).astype(F32))
        pooled.append((mean - u_new).astype(BF16))
    cols = slice(2 * pair * POOL_GROUP, 2 * (pair + 1) * POOL_GROUP)
    return (_dot(jnp.concatenate(pooled, axis=-1), pw_ref[pair]) + pb_ref[:, cols]) * ps_ref[:, cols]


def _pool_mixer(ext_ref, take_new, first_row, pw_ref, pb_ref, ps_ref):
    return jnp.concatenate([_pool_pair(ext_ref, take_new, first_row, pw_ref, pb_ref, ps_ref, i).astype(BF16)
                            for i in range(N_POOL_GROUPS // 2)], axis=-1)


def _fill_kv_variants(kx_ref, vx_ref, row0, kv):
    rows = kv.shape[0]
    low = lax.broadcasted_iota(jnp.int32, (rows, LANES), 1) < HEAD_DIM
    ones = (jnp.where(low, 1.0, 0.0).astype(BF16), jnp.where(low, 0.0, 1.0).astype(BF16))
    for t, base in ((0, 0), (1, D_KV)):
        both = kv[:, base:base + D_KV]
        swapped = pltpu.roll(both, HEAD_DIM, axis=1)
        variants = (jnp.where(low, both, 0.0), jnp.where(low, 0.0, swapped),
                    jnp.where(low, swapped, 0.0), jnp.where(low, 0.0, both))
        for i, val in enumerate(variants):
            if t == 0:
                kx_ref[i, row0:row0 + rows, :] = val.astype(BF16)
            else:
                vx_ref[i, row0:row0 + rows, :LANES] = val.astype(BF16)
                vx_ref[i, row0:row0 + rows, LANES:] = ones[i % 2]


def _sink_columns(sinks_ref, q_rows):
    m_rows = PAIRS_PER_KV * q_rows
    row_blk = lax.broadcasted_iota(jnp.int32, (m_rows, 1), 0) // q_rows
    cols = []
    for kh in range(N_KV_HEADS):
        pair = []
        for ab in range(2):
            sink = jnp.zeros((m_rows, 1), F32)
            for pp in range(PAIRS_PER_KV):
                sink = jnp.where(row_blk == pp, sinks_ref[kh * 2 * PAIRS_PER_KV + 2 * pp + ab] * LOG2E, sink)
            pair.append(sink)
        cols.append(pair)
    return cols


def _attend_block(q_ref, q_row0, q_rows, kx_ref, vx_ref, key_row0, n_keys, key_valid, sink_cols, yb_ref):
    m_rows = PAIRS_PER_KV * q_rows
    low = lax.broadcasted_iota(jnp.int32, (m_rows, LANES), 1) < HEAD_DIM
    for kh in range(N_KV_HEADS):
        qs = jnp.concatenate(
            [q_ref[q_row0:q_row0 + q_rows, (kh * PAIRS_PER_KV + pp) * LANES:(kh * PAIRS_PER_KV + pp + 1) * LANES]
             for pp in range(PAIRS_PER_KV)], axis=0)
        o = None
        row_max = []
        for ab in range(2):
            s = _dot_nt(qs, kx_ref[2 * kh + ab, key_row0:key_row0 + n_keys, :])
            if key_valid is not None:
                s = jnp.where(key_valid, s, NEG_BIG)
            m = jnp.maximum(jnp.max(s, axis=-1, keepdims=True), sink_cols[kh][ab])
            p = jnp.exp2(s - m).astype(BF16)
            pv = _dot(p, vx_ref[2 * kh + ab, key_row0:key_row0 + n_keys, :])
            o = pv if o is None else o + pv
            row_max.append(m)
        sink_p = jnp.exp2(jnp.where(low, sink_cols[kh][0], sink_cols[kh][1])
                          - jnp.where(low, row_max[0], row_max[1]))
        out = o[:, :LANES] / (o[:, LANES:] + sink_p)
        for pp in range(PAIRS_PER_KV):
            col = (kh * PAIRS_PER_KV + pp) * LANES
            yb_ref[q_row0:q_row0 + q_rows, col:col + LANES] = out[pp * q_rows:(pp + 1) * q_rows].astype(BF16)


DENSE_WEIGHTS = ("wg", "wpp", "wap", "wo", "w1", "w2")


class _LazyWeights:
    def __init__(self, hbm_refs, vmem_refs, sem):
        assert len(hbm_refs) == len(vmem_refs) == len(DENSE_WEIGHTS)
        self._copies = [pltpu.make_async_copy(src, dst, sem.at[k])
                        for k, (src, dst) in enumerate(zip(hbm_refs, vmem_refs))]
        self._waited = set()

    def start_all(self):
        for copy in self._copies:
            copy.start()

    def wait(self, name):
        k = DENSE_WEIGHTS.index(name)
        if k not in self._waited:
            self._copies[k].wait()
            self._waited.add(k)

    def before_ffn_chunk(self, c):
        if c == 0:
            self.wait("w1")

    def before_ffn_out(self):
        self.wait("w2")

    def finished(self):
        return len(self._waited) == len(self._copies)


def _dense_tail(x1, h, ya, yb, wg_ref, bg_ref, wpp_ref, wap_ref, wo_ref, g2_ref, w1_ref, w2_ref, a_scr, lazy=None):
    wait = lazy.wait if lazy is not None else (lambda name: None)
    wait("wg")
    gates = jax.nn.sigmoid(_dot(h, wg_ref[...]) + bg_ref[...])
    wait("wpp")
    a = _dot(ya, wpp_ref[...])
    wait("wap")
    b = _dot(yb, wap_ref[...])
    mixed = gates[:, :D_MODEL] * a + gates[:, D_MODEL:] * b
    wait("wo")
    x2 = x1 + _dot(mixed.astype(BF16), wo_ref[...])
    return _ffn(x2, g2_ref, w1_ref, w2_ref, a_scr, loader=lazy)


def _prompt_mix_tile(sinks_ref, x1_ref, h_ref, u_ref, q_ref, kv_ref, pw_ref, pb_ref, ps_ref, bg_ref, g2_ref,
                     dense_w, o_ref, uext, kx, vx, yb_scr, a_scr, lazy):
    rows = x1_ref.shape[0]
    t = pl.program_id(1)
    wg_ref, wpp_ref, wap_ref, wo_ref, w1_ref, w2_ref = dense_w

    @pl.when(t == 0)
    def _():
        uext[:POOL_HIST_PAD, :] = jnp.zeros((POOL_HIST_PAD, D_POOL), F32)
        kx[:, :WINDOW, :] = jnp.zeros((4, WINDOW, LANES), BF16)
        vx[:, :WINDOW, :] = jnp.zeros((4, WINDOW, 2 * LANES), BF16)

    uext[POOL_HIST_PAD:, :] = u_ref[...]
    _fill_kv_variants(kx, vx, WINDOW, kv_ref[...])

    ya = _pool_mixer(uext, lambda a: a[POOL_HIST_PAD:], t * rows, pw_ref, pb_ref, ps_ref)

    n_chunks = rows // CHUNK
    n_keys = WINDOW + CHUNK
    key_chunk = lax.broadcasted_iota(jnp.int32, (1, n_keys), 1) // CHUNK
    sink_cols = _sink_columns(sinks_ref, CHUNK)
    for j in range(n_chunks):
        first_chunk = t * n_chunks + j - WINDOW // CHUNK
        key_valid = key_chunk + first_chunk >= 0 if j < WINDOW // CHUNK else None
        _attend_block(q_ref, j * CHUNK, CHUNK, kx, vx, j * CHUNK, n_keys, key_valid, sink_cols, yb_scr)

    o_ref[...] = _dense_tail(x1_ref[...], h_ref[...], ya, yb_scr[...], wg_ref, bg_ref, wpp_ref, wap_ref, wo_ref,
                             g2_ref, w1_ref, w2_ref, a_scr, lazy)

    uext[:POOL_HIST_PAD, :] = uext[rows:rows + POOL_HIST_PAD, :]
    kx[:, :WINDOW, :] = kx[:, rows:rows + WINDOW, :]
    vx[:, :WINDOW, :] = vx[:, rows:rows + WINDOW, :]


def _prompt_mix_body(*refs):
    n_small = 11
    n_w = len(DENSE_WEIGHTS)
    _prompt_mix_tile(*refs[:n_small], refs[n_small:n_small + n_w], *refs[n_small + n_w:], None)


def _sample_tail_body(*refs, n_streams, t_new):
    n_w = len(DENSE_WEIGHTS)
    (sinks_ref, x1_ref, h_ref, u_ref, q_ref, kv_ref, hu_ref, hkv_ref,
     pw_ref, pb_ref, ps_ref, bg_ref, g2_ref) = refs[:13]
    dense_hbm = refs[13:13 + n_w]
    o_ref, uext, kx, vx, yb_scr, a_scr = refs[13 + n_w:19 + n_w]
    wg_ref, wpp_ref, wap_ref, wo_ref, w1_ref, w2_ref = dense_vmem = refs[19 + n_w:19 + 2 * n_w]
    lazy = _LazyWeights(dense_hbm, dense_vmem, refs[19 + 2 * n_w])
    lazy.start_all()

    seg = POOL_HIST_PAD + t_new
    for e in range(n_streams):
        uext[e * seg:e * seg + POOL_HIST_PAD, :] = hu_ref[e]
        uext[e * seg + POOL_HIST_PAD:(e + 1) * seg, :] = u_ref[e * t_new:(e + 1) * t_new, :]

    def take_new(a):
        return jnp.concatenate([a[e * seg + POOL_HIST_PAD:(e + 1) * seg] for e in range(n_streams)], axis=0)

    ya = _pool_mixer(uext, take_new, None, pw_ref, pb_ref, ps_ref)

    sink_cols = _sink_columns(sinks_ref, t_new)
    for e in range(n_streams):
        kx_e, vx_e = kx.at[e % 2], vx.at[e % 2]
        _fill_kv_variants(kx_e, vx_e, 0, hkv_ref[e])
        _fill_kv_variants(kx_e, vx_e, WINDOW, kv_ref[e * t_new:(e + 1) * t_new, :])
        _attend_block(q_ref, e * t_new, t_new, kx_e, vx_e, 0, WINDOW + t_new, None, sink_cols, yb_scr)

    o_ref[...] = _dense_tail(x1_ref[...], h_ref[...], ya, yb_scr[...], wg_ref, bg_ref, wpp_ref, wap_ref, wo_ref,
                             g2_ref, w1_ref, w2_ref, a_scr, lazy)
    assert lazy.finished()


def _const_spec(shape):
    return pl.BlockSpec(shape, lambda *_: (0,) * len(shape), pipeline_mode=pl.Buffered(1))


def _front_call(xp2d, xs2d, g1, w1, w2, gm, win, qn2, kn2, cast_srcs):
    n_p = xp2d.shape[0]
    tm = ROW_TILE
    assert n_p % tm == 0 and xs2d.shape[0] == tm
    n_prompt_steps = n_p // tm
    tiles = n_prompt_steps + 1
    n = n_p + tm
    tile_of = lambda i: jnp.maximum(i - 1, 0)
    row = lambda width: pl.BlockSpec((tm, width), lambda i: (tile_of(i), 0))
    cast_in, cast_out, cast_shapes = [], [], []
    for w, cols in cast_srcs:
        _, w_rows, w_cols = w.shape
        n_blk = max(d for d in range(1, tiles + 1) if w_rows % (BF16_SUBLANES * d) == 0)
        rows = w_rows // n_blk
        assert cols % LANES == 0
        cast_in.append(pl.BlockSpec((None, rows, w_cols), lambda i, n_blk=n_blk: (0, jnp.minimum(tile_of(i), n_blk - 1), 0)))
        cast_out.append(pl.BlockSpec((rows, cols), lambda i, n_blk=n_blk: (jnp.minimum(tile_of(i), n_blk - 1), 0)))
        cast_shapes.append(jax.ShapeDtypeStruct((w_rows, cols), BF16))
    hbm = pl.BlockSpec(memory_space=pl.ANY)
    stage_shapes = [(WEIGHT_STAGE_SLOTS, rows, w.shape[2]) for rows, w in zip(WEIGHT_STAGE_ROWS, (w1, w2, win))]
    return pl.pallas_call(
        functools.partial(_front_body, n_cast=len(cast_srcs), n_prompt_steps=n_prompt_steps),
        grid=(tiles + 1,),
        in_specs=[pl.BlockSpec((tm, D_MODEL), lambda i: (jnp.minimum(tile_of(i), n_prompt_steps - 1), 0)),
                  _const_spec(xs2d.shape),
                  _const_spec(g1.shape), hbm, hbm, _const_spec(gm.shape), hbm,
                  _const_spec(qn2.shape), _const_spec(kn2.shape)]
                 + cast_in,
        out_specs=[row(D_MODEL), row(D_MODEL), row(D_POOL), row(D_Q), row(2 * D_KV)] + cast_out,
        out_shape=[jax.ShapeDtypeStruct((n, D_MODEL), F32), jax.ShapeDtypeStruct((n, D_MODEL), BF16),
                   jax.ShapeDtypeStruct((n, D_POOL), F32), jax.ShapeDtypeStruct((n, D_Q), BF16),
                   jax.ShapeDtypeStruct((n, 2 * D_KV), F32)] + cast_shapes,
        scratch_shapes=[pltpu.VMEM((tm, D_FF), BF16),
                        pltpu.VMEM((D_MODEL, 2 * D_FF), BF16), pltpu.VMEM((D_FF, D_MODEL), BF16),
                        pltpu.VMEM((D_MODEL, D_UQKV), BF16)]
                       + [pltpu.VMEM(shape, F32) for shape in stage_shapes]
                       + [pltpu.SemaphoreType.DMA((WEIGHT_STAGE_SLOTS,)) for _ in stage_shapes],
        compiler_params=pltpu.CompilerParams(dimension_semantics=("arbitrary",),
                                             vmem_limit_bytes=VMEM_LIMIT_BYTES),
        name="front",
    )(xp2d, xs2d, g1, w1, w2, gm, win, qn2, kn2, *[w for w, _ in cast_srcs])


def _dense_weight_scratch(dense_w):
    return ([pltpu.VMEM(w.shape, w.dtype) for w in dense_w] + [pltpu.SemaphoreType.DMA((len(dense_w),))])


def _prompt_mix_call(sinks, x1, h, u, q, kv, small_w, dense_w, n_seq, seq_len):
    tm = ROW_TILE
    assert seq_len % tm == 0 and len(small_w) == 5 and len(dense_w) == len(DENSE_WEIGHTS)
    tiles = seq_len // tm
    row = lambda width: pl.BlockSpec((tm, width), lambda b, t: (b * tiles + t, 0))
    return pl.pallas_call(
        _prompt_mix_body,
        grid=(n_seq, tiles),
        in_specs=[pl.BlockSpec(memory_space=pltpu.SMEM),
                  row(D_MODEL), row(D_MODEL), row(D_POOL), row(D_Q), row(2 * D_KV)]
                 + [_const_spec(w.shape) for w in tuple(small_w) + tuple(dense_w)],
        out_specs=row(D_MODEL),
        out_shape=jax.ShapeDtypeStruct((n_seq * seq_len, D_MODEL), F32),
        scratch_shapes=[pltpu.VMEM((POOL_HIST_PAD + tm, D_POOL), F32),
                        pltpu.VMEM((4, WINDOW + tm, LANES), BF16),
                        pltpu.VMEM((4, WINDOW + tm, 2 * LANES), BF16),
                        pltpu.VMEM((tm, D_Q), BF16),
                        pltpu.VMEM((tm, D_FF), BF16)],
        compiler_params=pltpu.CompilerParams(dimension_semantics=("arbitrary", "arbitrary"),
                                             vmem_limit_bytes=VMEM_LIMIT_BYTES),
        name="prompt_mix",
    )(sinks, x1, h, u, q, kv, *small_w, *dense_w)


def _sample_tail_call(sinks, x1, h, u, q, kv, hist_u, hist_kv, small_w, dense_w, row_block, n_streams, t_new):
    tm = ROW_TILE
    assert n_streams * t_new == tm and t_new % BF16_SUBLANES == 0 and len(dense_w) == len(DENSE_WEIGHTS)
    row = lambda width: pl.BlockSpec((tm, width), lambda i: (row_block, 0))
    return pl.pallas_call(
        functools.partial(_sample_tail_body, n_streams=n_streams, t_new=t_new),
        grid=(1,),
        in_specs=[pl.BlockSpec(memory_space=pltpu.SMEM),
                  row(D_MODEL), row(D_MODEL), row(D_POOL), row(D_Q), row(2 * D_KV),
                  _const_spec(hist_u.shape), _const_spec(hist_kv.shape)]
                 + [_const_spec(w.shape) for w in small_w]
                 + [pl.BlockSpec(memory_space=pl.ANY) for _ in dense_w],
        out_specs=pl.BlockSpec((tm, D_MODEL), lambda i: (0, 0)),
        out_shape=jax.ShapeDtypeStruct((tm, D_MODEL), F32),
        scratch_shapes=[pltpu.VMEM((n_streams * (POOL_HIST_PAD + t_new), D_POOL), F32),
                        pltpu.VMEM((2, 4, WINDOW + t_new, LANES), BF16),
                        pltpu.VMEM((2, 4, WINDOW + t_new, 2 * LANES), BF16),
                        pltpu.VMEM((tm, D_Q), BF16),
                        pltpu.VMEM((tm, D_FF), BF16)] + _dense_weight_scratch(dense_w),
        compiler_params=pltpu.CompilerParams(dimension_semantics=("arbitrary",),
                                             vmem_limit_bytes=VMEM_LIMIT_BYTES),
        name="sample_tail",
    )(sinks, x1, h, u, q, kv, hist_u, hist_kv, *small_w, *dense_w)


def kernel(x_prompt, x_sample, state_pool, cache_k, cache_v, norm_ffn1, ffn1_w_in, ffn1_w_out, norm_mix, w_in, b_gate, pool_w, pool_b, pool_scale, q_norm, k_norm, sinks, w_pool_proj, w_attn_proj, w_out, norm_ffn2, ffn2_w_in, ffn2_w_out):
    depth = norm_ffn1.shape[0]
    assert depth == 1, "single-layer trunk"
    bp, sp, _ = x_prompt.shape
    bs, ts, _ = x_sample.shape
    n_p = bp * sp

    front_w = (norm_ffn1[0].reshape(1, D_MODEL), ffn1_w_in, ffn1_w_out, norm_mix[0].reshape(1, D_MODEL), w_in,
               jnp.tile(q_norm[0], 2).reshape(1, LANES), jnp.tile(k_norm[0], 2).reshape(1, LANES))
    cast_srcs = ((w_in, 2 * D_MODEL), (w_pool_proj, D_MODEL), (w_attn_proj, D_MODEL), (w_out, D_MODEL),
                 (ffn2_w_in, 2 * D_FF), (ffn2_w_out, D_MODEL))
    x1, h, u, q, kv, wg, wpp, wap, wo, ffn2_w1, ffn2_w2 = _front_call(
        x_prompt.reshape(n_p, D_MODEL), x_sample.reshape(bs * ts, D_MODEL), *front_w, cast_srcs=cast_srcs)

    pw = pool_w[0].astype(BF16)
    zero_blk = jnp.zeros((POOL_GROUP, POOL_GROUP), BF16)
    pw_pairs = jnp.stack([jnp.block([[pw[2 * i], zero_blk], [zero_blk, pw[2 * i + 1]]])
                          for i in range(N_POOL_GROUPS // 2)])
    small_w = (pw_pairs, pool_b[0].reshape(1, D_POOL), pool_scale[0].reshape(1, D_POOL),
               b_gate[0].reshape(1, 2 * D_MODEL), norm_ffn2[0].reshape(1, D_MODEL))
    dense_w = (wg, wpp, wap, wo, ffn2_w1, ffn2_w2)
    sink_vec = sinks[0]

    y_prompt = _prompt_mix_call(sink_vec, x1, h, u, q, kv, small_w, dense_w, bp, sp).reshape(bp, sp, D_MODEL)
    new_pool_p = jnp.stack([u[(b + 1) * sp - POOL_HIST:(b + 1) * sp] for b in range(bp)])
    kv_last = jnp.stack([kv[(b + 1) * sp - WINDOW:(b + 1) * sp] for b in range(bp)])
    new_k_p = kv_last[..., :D_KV].reshape(bp, WINDOW, N_KV_HEADS, HEAD_DIM)
    new_v_p = kv_last[..., D_KV:].reshape(bp, WINDOW, N_KV_HEADS, HEAD_DIM)

    hist_u = jnp.pad(state_pool[0], ((0, 0), (POOL_HIST_PAD - POOL_HIST, 0), (0, 0)))
    hist_kv = jnp.concatenate([cache_k[0].reshape(bs, WINDOW, D_KV), cache_v[0].reshape(bs, WINDOW, D_KV)], axis=-1)
    y_sample = _sample_tail_call(sink_vec, x1, h, u, q, kv, hist_u, hist_kv, small_w, dense_w,
                                 n_p // ROW_TILE, bs, ts).reshape(bs, ts, D_MODEL)
    us3 = u[n_p:].reshape(bs, ts, D_POOL)
    kvs3 = kv[n_p:].reshape(bs, ts, 2 * D_KV)
    new_pool_s = jnp.concatenate([state_pool[0], us3], axis=1)[:, -POOL_HIST:]
    kv_full = jnp.concatenate([hist_kv, kvs3], axis=1)[:, -WINDOW:]
    new_k_s = kv_full[..., :D_KV].reshape(bs, WINDOW, N_KV_HEADS, HEAD_DIM)
    new_v_s = kv_full[..., D_KV:].reshape(bs, WINDOW, N_KV_HEADS, HEAD_DIM)

    return (y_prompt, y_sample, new_pool_p[None], new_k_p[None], new_v_p[None],
            new_pool_s[None], new_k_s[None], new_v_s[None])
```

```python
import functools

import jax
import jax.numpy as jnp
from jax import lax
from jax.experimental import pallas as pl
from jax.experimental.pallas import tpu as pltpu

D_MODEL = 1024
CHUNK = 64
N_HEADS = 16
N_KV_HEADS = 2
HEAD_DIM = 64
WINDOW = 128
D_POOL = 512
N_POOL_GROUPS = 4
POOL_GROUP = 128
POOL_WINDOWS = (2, 4, 8, 16)
POOL_HIST = 15
POOL_HIST_PAD = 16
D_Q = N_HEADS * HEAD_DIM
D_KV = N_KV_HEADS * HEAD_DIM
D_UQKV = D_POOL + D_Q + 2 * D_KV
D_FF = 2816
EPS = 1e-6

LANES = 128
BF16_SUBLANES = 16
FF_CHUNK = 256
N_FF_CHUNKS = D_FF // FF_CHUNK
PAIRS_PER_KV = (N_HEADS // N_KV_HEADS) // 2
ROW_TILE = 512
PROJ_BLOCK_ORDER = (2, 3, 4, 5, 6, 0, 1)
WEIGHT_STAGE_SLOTS = (3, 2)
VMEM_LIMIT_BYTES = 58 * 1024 * 1024
NEG_BIG = float(jnp.finfo(jnp.float32).min)
LOG2E = 1.4426950408889634

BF16 = jnp.bfloat16
F32 = jnp.float32


def _dot(a, b):
    return jnp.dot(a, b, preferred_element_type=F32)


def _dot_nt(a, b):
    return lax.dot_general(a, b, (((1,), (1,)), ((), ())), preferred_element_type=F32)


def _rms_scale(x):
    return lax.rsqrt(jnp.mean(x * x, axis=-1, keepdims=True) + EPS)


def _ffn(x, g_ref, w1_ref, w2_ref, a_scr, side_work=(), loader=None):
    xg = (x * g_ref[...]).astype(BF16)
    r = _rms_scale(x)
    for c in range(N_FF_CHUNKS):
        if loader is not None:
            loader.before_ffn_chunk(c)
        gate = _dot(xg, w1_ref[:, c * FF_CHUNK:(c + 1) * FF_CHUNK]) * r
        up = _dot(xg, w1_ref[:, D_FF + c * FF_CHUNK:D_FF + (c + 1) * FF_CHUNK]) * r
        a_scr[:, c * FF_CHUNK:(c + 1) * FF_CHUNK] = (gate * jax.nn.sigmoid(gate) * up).astype(BF16)
        if c < len(side_work):
            side_work[c]()
    assert len(side_work) <= N_FF_CHUNKS
    if loader is not None:
        loader.before_ffn_out()
    return x + 0.5 * _dot(a_scr[...], w2_ref[...])


def _half_sumsq(x):
    y = x * x
    low = lax.broadcasted_iota(jnp.int32, y.shape, 1) < HEAD_DIM
    sum_low = jnp.sum(jnp.where(low, y, 0.0), axis=-1, keepdims=True)
    sum_high = jnp.sum(jnp.where(low, 0.0, y), axis=-1, keepdims=True)
    return jnp.where(low, sum_low, sum_high)


def _head_norm(x, g2):
    return x * lax.rsqrt(_half_sumsq(x) * (1.0 / HEAD_DIM) + EPS) * g2


def _cast_block(src_ref, dst_ref):
    dst_ref[...] = src_ref[:, src_ref.shape[1] - dst_ref.shape[1]:].astype(BF16)


class _WeightLoader:
    def __init__(self, w1_hbm, w2_hbm, win_hbm, w1_bf, w2_bf, win_bf, stage_cols, stage_rows, sem_cols, sem_rows):
        blk = FF_CHUNK
        self._units = {"cols": [], "rows": []}
        self._stage = {"cols": stage_cols, "rows": stage_rows}
        self._sem = {"cols": sem_cols, "rows": sem_rows}
        self._done = {"cols": 0, "rows": 0}

        def col_unit(hbm, dst, col):
            return (hbm.at[0, :, pl.ds(col, blk)], dst, (slice(None), slice(col, col + blk)))

        for c in range(N_FF_CHUNKS):
            self._units["cols"].append(col_unit(w1_hbm, w1_bf, c * blk))
            self._units["cols"].append(col_unit(w1_hbm, w1_bf, D_FF + c * blk))
            self._units["rows"].append((w2_hbm.at[0, pl.ds(c * blk, blk), :], w2_bf,
                                        (slice(c * blk, (c + 1) * blk), slice(None))))
        self._n_ffn_cols = len(self._units["cols"])
        for b in PROJ_BLOCK_ORDER:
            self._units["cols"].append(col_unit(win_hbm, win_bf, b * blk))

    def _copy(self, kind, k):
        n_slots = self._stage[kind].shape[0]
        src, _, _ = self._units[kind][k]
        return pltpu.make_async_copy(src, self._stage[kind].at[k % n_slots], self._sem[kind].at[k % n_slots])

    def start_all(self):
        for kind in ("cols", "rows"):
            for k in range(min(self._stage[kind].shape[0], len(self._units[kind]))):
                self._copy(kind, k).start()

    def _land(self, kind, upto):
        n_slots = self._stage[kind].shape[0]
        for k in range(self._done[kind], upto):
            self._copy(kind, k).wait()
            _, dst, idx = self._units[kind][k]
            dst[idx] = self._stage[kind][k % n_slots].astype(BF16)
            if k + n_slots < len(self._units[kind]):
                self._copy(kind, k + n_slots).start()
        self._done[kind] = max(self._done[kind], upto)

    def before_ffn_chunk(self, c):
        self._land("cols", 2 * (c + 1))
        self._land("rows", c)

    def before_ffn_out(self):
        self._land("rows", N_FF_CHUNKS)

    def before_proj_blocks(self, n_used):
        self._land("cols", self._n_ffn_cols + n_used)

    def finished(self):
        return all(self._done[kind] == len(self._units[kind]) for kind in self._units)


def _front_compute(x, g1_ref, w1_bf, w2_bf, gm_ref, win_bf, qn_ref, kn_ref,
                   x1_ref, h_ref, u_ref, q_ref, kv_ref, a_scr, casts, loader):
    x1 = _ffn(x, g1_ref, w1_bf, w2_bf, a_scr, casts, loader)
    x1_ref[...] = x1
    xg = x1 * gm_ref[...]
    r = _rms_scale(x1)
    h_ref[...] = (xg * r).astype(BF16)
    h = xg.astype(BF16)
    qg = qn_ref[...] * (HEAD_DIM ** -0.5 * LOG2E)
    blk = 2 * LANES
    for b in range(D_Q // blk):
        if loader is not None:
            loader.before_proj_blocks(b + 1)
        zq = _dot(h, win_bf[:, D_POOL + b * blk:D_POOL + (b + 1) * blk]) * r
        for p in range(2):
            q_ref[:, b * blk + p * LANES:b * blk + (p + 1) * LANES] = _head_norm(
                zq[:, p * LANES:(p + 1) * LANES], qg).astype(BF16)
    if loader is not None:
        loader.before_proj_blocks(D_Q // blk + 1)
    zkv = _dot(h, win_bf[:, D_POOL + D_Q:D_POOL + D_Q + 2 * D_KV]) * r
    kv_ref[:, :D_KV] = _head_norm(zkv[:, :D_KV], kn_ref[...])
    kv_ref[:, D_KV:] = zkv[:, D_KV:]
    if loader is not None:
        loader.before_proj_blocks(len(PROJ_BLOCK_ORDER))
    u_ref[...] = _dot(h, win_bf[:, :D_POOL]) * r


def _front_body(*refs, n_cast):
    xp_ref, xs_ref, g1_ref, w1_hbm, w2_hbm, gm_ref, win_hbm, qn_ref, kn_ref = refs[:9]
    cast_src = refs[9:9 + n_cast]
    outs = refs[9 + n_cast:14 + n_cast]
    cast_dst = refs[14 + n_cast:14 + 2 * n_cast]
    a_scr, w1_bf, w2_bf, win_bf, stage_cols, stage_rows, sem_cols, sem_rows = refs[14 + 2 * n_cast:]
    casts = [functools.partial(_cast_block, src, dst) for src, dst in zip(cast_src, cast_dst)]
    i = pl.program_id(0)
    weights = (g1_ref, w1_bf, w2_bf, gm_ref, win_bf, qn_ref, kn_ref)

    @pl.when(i == 0)
    def _():
        loader = _WeightLoader(w1_hbm, w2_hbm, win_hbm, w1_bf, w2_bf, win_bf,
                               stage_cols, stage_rows, sem_cols, sem_rows)
        loader.start_all()
        _front_compute(xs_ref[...], *weights, *outs, a_scr, casts, loader)
        assert loader.finished()

    @pl.when(i > 0)
    def _():
        _front_compute(xp_ref[...], *weights, *outs, a_scr, casts, None)


def _pool_pair(ext_ref, take_new, first_row, pw_ref, pb_ref, ps_ref, pair):
    pooled = []
    for gi in (2 * pair, 2 * pair + 1):
        w = POOL_WINDOWS[gi]
        ext = ext_ref[:, gi * POOL_GROUP:(gi + 1) * POOL_GROUP]
        u_new = take_new(ext)
        s = ext
        k = 1
        while k < w:
            s = s + pltpu.roll(s, k, axis=0)
            k *= 2
        s = take_new(s)
        if first_row is None:
            mean = s * (1.0 / w)
        else:
            pos = first_row + lax.broadcasted_iota(jnp.int32, (s.shape[0], 1), 0)
            mean = s * (1.0 / jnp.minimum(pos + 1, w).astype(F32))
        pooled.append((mean - u_new).astype(BF16))
    cols = slice(2 * pair * POOL_GROUP, 2 * (pair + 1) * POOL_GROUP)
    return (_dot(jnp.concatenate(pooled, axis=-1), pw_ref[pair]) + pb_ref[:, cols]) * ps_ref[:, cols]


def _pool_mixer(ext_ref, take_new, first_row, pw_ref, pb_ref, ps_ref):
    return jnp.concatenate([_pool_pair(ext_ref, take_new, first_row, pw_ref, pb_ref, ps_ref, i).astype(BF16)
                            for i in range(N_POOL_GROUPS // 2)], axis=-1)


def _fill_kv_variants(kx_ref, vx_ref, row0, kv):
    rows = kv.shape[0]
    low = lax.broadcasted_iota(jnp.int32, (rows, LANES), 1) < HEAD_DIM
    ones = (jnp.where(low, 1.0, 0.0).astype(BF16), jnp.where(low, 0.0, 1.0).astype(BF16))
    for t, base in ((0, 0), (1, D_KV)):
        both = kv[:, base:base + D_KV]
        swapped = pltpu.roll(both, HEAD_DIM, axis=1)
        variants = (jnp.where(low, both, 0.0), jnp.where(low, 0.0, swapped),
                    jnp.where(low, swapped, 0.0), jnp.where(low, 0.0, both))
        for i, val in enumerate(variants):
            if t == 0:
                kx_ref[i, row0:row0 + rows, :] = val.astype(BF16)
            else:
                vx_ref[i, row0:row0 + rows, :LANES] = val.astype(BF16)
                vx_ref[i, row0:row0 + rows, LANES:] = ones[i % 2]


def _sink_columns(sinks_ref, q_rows):
    m_rows = PAIRS_PER_KV * q_rows
    row_blk = lax.broadcasted_iota(jnp.int32, (m_rows, 1), 0) // q_rows
    cols = []
    for kh in range(N_KV_HEADS):
        pair = []
        for ab in range(2):
            sink = jnp.zeros((m_rows, 1), F32)
            for pp in range(PAIRS_PER_KV):
                sink = jnp.where(row_blk == pp, sinks_ref[kh * 2 * PAIRS_PER_KV + 2 * pp + ab] * LOG2E, sink)
            pair.append(sink)
        cols.append(pair)
    return cols


def _attend_block(q_ref, q_row0, q_rows, kx_ref, vx_ref, key_row0, n_keys, key_valid, sink_cols, yb_ref):
    m_rows = PAIRS_PER_KV * q_rows
    low = lax.broadcasted_iota(jnp.int32, (m_rows, LANES), 1) < HEAD_DIM
    for kh in range(N_KV_HEADS):
        qs = jnp.concatenate(
            [q_ref[q_row0:q_row0 + q_rows, (kh * PAIRS_PER_KV + pp) * LANES:(kh * PAIRS_PER_KV + pp + 1) * LANES]
             for pp in range(PAIRS_PER_KV)], axis=0)
        o = None
        row_max = []
        for ab in range(2):
            s = _dot_nt(qs, kx_ref[2 * kh + ab, key_row0:key_row0 + n_keys, :])
            if key_valid is not None:
                s = jnp.where(key_valid, s, NEG_BIG)
            m = jnp.maximum(jnp.max(s, axis=-1, keepdims=True), sink_cols[kh][ab])
            p = jnp.exp2(s - m).astype(BF16)
            pv = _dot(p, vx_ref[2 * kh + ab, key_row0:key_row0 + n_keys, :])
            o = pv if o is None else o + pv
            row_max.append(m)
        sink_p = jnp.exp2(jnp.where(low, sink_cols[kh][0], sink_cols[kh][1])
                          - jnp.where(low, row_max[0], row_max[1]))
        out = o[:, :LANES] / (o[:, LANES:] + sink_p)
        for pp in range(PAIRS_PER_KV):
            col = (kh * PAIRS_PER_KV + pp) * LANES
            yb_ref[q_row0:q_row0 + q_rows, col:col + LANES] = out[pp * q_rows:(pp + 1) * q_rows].astype(BF16)


DENSE_WEIGHTS = ("wg", "wpp", "wap", "wo", "w1", "w2")


class _LazyWeights:
    def __init__(self, hbm_refs, vmem_refs, sem):
        assert len(hbm_refs) == len(vmem_refs) == len(DENSE_WEIGHTS)
        self._copies = [pltpu.make_async_copy(src, dst, sem.at[k])
                        for k, (src, dst) in enumerate(zip(hbm_refs, vmem_refs))]
        self._waited = set()

    def start_all(self):
        for copy in self._copies:
            copy.start()

    def wait(self, name):
        k = DENSE_WEIGHTS.index(name)
        if k not in self._waited:
            self._copies[k].wait()
            self._waited.add(k)

    def before_ffn_chunk(self, c):
        if c == 0:
            self.wait("w1")

    def before_ffn_out(self):
        self.wait("w2")

    def finished(self):
        return len(self._waited) == len(self._copies)


def _dense_tail(x1, h, ya, yb, wg_ref, bg_ref, wpp_ref, wap_ref, wo_ref, g2_ref, w1_ref, w2_ref, a_scr, lazy=None):
    wait = lazy.wait if lazy is not None else (lambda name: None)
    wait("wg")
    gates = jax.nn.sigmoid(_dot(h, wg_ref[...]) + bg_ref[...])
    wait("wpp")
    a = _dot(ya, wpp_ref[...])
    wait("wap")
    b = _dot(yb, wap_ref[...])
    mixed = gates[:, :D_MODEL] * a + gates[:, D_MODEL:] * b
    wait("wo")
    x2 = x1 + _dot(mixed.astype(BF16), wo_ref[...])
    return _ffn(x2, g2_ref, w1_ref, w2_ref, a_scr, loader=lazy)


def _prompt_mix_tile(sinks_ref, x1_ref, h_ref, u_ref, q_ref, kv_ref, pw_ref, pb_ref, ps_ref, bg_ref, g2_ref,
                     dense_w, o_ref, uext, kx, vx, yb_scr, a_scr, lazy):
    rows = x1_ref.shape[0]
    t = pl.program_id(1)
    wg_ref, wpp_ref, wap_ref, wo_ref, w1_ref, w2_ref = dense_w

    @pl.when(t == 0)
    def _():
        uext[:POOL_HIST_PAD, :] = jnp.zeros((POOL_HIST_PAD, D_POOL), F32)
        kx[:, :WINDOW, :] = jnp.zeros((4, WINDOW, LANES), BF16)
        vx[:, :WINDOW, :] = jnp.zeros((4, WINDOW, 2 * LANES), BF16)

    uext[POOL_HIST_PAD:, :] = u_ref[...]
    _fill_kv_variants(kx, vx, WINDOW, kv_ref[...])

    ya = _pool_mixer(uext, lambda a: a[POOL_HIST_PAD:], t * rows, pw_ref, pb_ref, ps_ref)

    n_chunks = rows // CHUNK
    n_keys = WINDOW + CHUNK
    key_chunk = lax.broadcasted_iota(jnp.int32, (1, n_keys), 1) // CHUNK
    sink_cols = _sink_columns(sinks_ref, CHUNK)
    for j in range(n_chunks):
        first_chunk = t * n_chunks + j - WINDOW // CHUNK
        key_valid = key_chunk + first_chunk >= 0 if j < WINDOW // CHUNK else None
        _attend_block(q_ref, j * CHUNK, CHUNK, kx, vx, j * CHUNK, n_keys, key_valid, sink_cols, yb_scr)

    o_ref[...] = _dense_tail(x1_ref[...], h_ref[...], ya, yb_scr[...], wg_ref, bg_ref, wpp_ref, wap_ref, wo_ref,
                             g2_ref, w1_ref, w2_ref, a_scr, lazy)

    uext[:POOL_HIST_PAD, :] = uext[rows:rows + POOL_HIST_PAD, :]
    kx[:, :WINDOW, :] = kx[:, rows:rows + WINDOW, :]
    vx[:, :WINDOW, :] = vx[:, rows:rows + WINDOW, :]


def _prompt_mix_body(*refs):
    n_small = 11
    n_w = len(DENSE_WEIGHTS)
    _prompt_mix_tile(*refs[:n_small], refs[n_small:n_small + n_w], *refs[n_small + n_w:], None)


def _sample_tail_body(*refs, n_streams, t_new):
    n_w = len(DENSE_WEIGHTS)
    (sinks_ref, x1_ref, h_ref, u_ref, q_ref, kv_ref, hu_ref, hkv_ref,
     pw_ref, pb_ref, ps_ref, bg_ref, g2_ref) = refs[:13]
    dense_hbm = refs[13:13 + n_w]
    o_ref, uext, kx, vx, yb_scr, a_scr = refs[13 + n_w:19 + n_w]
    wg_ref, wpp_ref, wap_ref, wo_ref, w1_ref, w2_ref = dense_vmem = refs[19 + n_w:19 + 2 * n_w]
    lazy = _LazyWeights(dense_hbm, dense_vmem, refs[19 + 2 * n_w])
    lazy.start_all()

    seg = POOL_HIST_PAD + t_new
    for e in range(n_streams):
        uext[e * seg:e * seg + POOL_HIST_PAD, :] = hu_ref[e]
        uext[e * seg + POOL_HIST_PAD:(e + 1) * seg, :] = u_ref[e * t_new:(e + 1) * t_new, :]

    def take_new(a):
        return jnp.concatenate([a[e * seg + POOL_HIST_PAD:(e + 1) * seg] for e in range(n_streams)], axis=0)

    ya = _pool_mixer(uext, take_new, None, pw_ref, pb_ref, ps_ref)

    sink_cols = _sink_columns(sinks_ref, t_new)
    for e in range(n_streams):
        kx_e, vx_e = kx.at[e % 2], vx.at[e % 2]
        _fill_kv_variants(kx_e, vx_e, 0, hkv_ref[e])
        _fill_kv_variants(kx_e, vx_e, WINDOW, kv_ref[e * t_new:(e + 1) * t_new, :])
        _attend_block(q_ref, e * t_new, t_new, kx_e, vx_e, 0, WINDOW + t_new, None, sink_cols, yb_scr)

    o_ref[...] = _dense_tail(x1_ref[...], h_ref[...], ya, yb_scr[...], wg_ref, bg_ref, wpp_ref, wap_ref, wo_ref,
                             g2_ref, w1_ref, w2_ref, a_scr, lazy)
    assert lazy.finished()


def _const_spec(shape):
    return pl.BlockSpec(shape, lambda *_: (0,) * len(shape), pipeline_mode=pl.Buffered(1))


def _front_call(xp2d, xs2d, g1, w1, w2, gm, win, qn2, kn2, cast_srcs):
    n_p = xp2d.shape[0]
    tm = ROW_TILE
    assert n_p % tm == 0 and xs2d.shape[0] == tm
    n_prompt_steps = n_p // tm
    steps = n_prompt_steps + 1
    n = n_p + tm
    row = lambda width: pl.BlockSpec((tm, width), lambda i: (jnp.where(i == 0, n_prompt_steps, i - 1), 0))
    cast_in, cast_out, cast_shapes = [], [], []
    for w, cols in cast_srcs:
        _, w_rows, w_cols = w.shape
        n_blk = max(d for d in range(1, steps + 1) if w_rows % (BF16_SUBLANES * d) == 0)
        rows = w_rows // n_blk
        assert cols % LANES == 0
        cast_in.append(pl.BlockSpec((None, rows, w_cols), lambda i, n_blk=n_blk: (0, jnp.minimum(i, n_blk - 1), 0)))
        cast_out.append(pl.BlockSpec((rows, cols), lambda i, n_blk=n_blk: (jnp.minimum(i, n_blk - 1), 0)))
        cast_shapes.append(jax.ShapeDtypeStruct((w_rows, cols), BF16))
    hbm = pl.BlockSpec(memory_space=pl.ANY)
    col_slots, row_slots = WEIGHT_STAGE_SLOTS
    return pl.pallas_call(
        functools.partial(_front_body, n_cast=len(cast_srcs)),
        grid=(steps,),
        in_specs=[pl.BlockSpec((tm, D_MODEL), lambda i: (jnp.maximum(i - 1, 0), 0)),
                  _const_spec(xs2d.shape),
                  _const_spec(g1.shape), hbm, hbm, _const_spec(gm.shape), hbm,
                  _const_spec(qn2.shape), _const_spec(kn2.shape)]
                 + cast_in,
        out_specs=[row(D_MODEL), row(D_MODEL), row(D_POOL), row(D_Q), row(2 * D_KV)] + cast_out,
        out_shape=[jax.ShapeDtypeStruct((n, D_MODEL), F32), jax.ShapeDtypeStruct((n, D_MODEL), BF16),
                   jax.ShapeDtypeStruct((n, D_POOL), F32), jax.ShapeDtypeStruct((n, D_Q), BF16),
                   jax.ShapeDtypeStruct((n, 2 * D_KV), F32)] + cast_shapes,
        scratch_shapes=[pltpu.VMEM((tm, D_FF), BF16),
                        pltpu.VMEM((D_MODEL, 2 * D_FF), BF16), pltpu.VMEM((D_FF, D_MODEL), BF16),
                        pltpu.VMEM((D_MODEL, D_UQKV), BF16),
                        pltpu.VMEM((col_slots, D_MODEL, FF_CHUNK), F32), pltpu.VMEM((row_slots, FF_CHUNK, D_MODEL), F32),
                        pltpu.SemaphoreType.DMA((col_slots,)), pltpu.SemaphoreType.DMA((row_slots,))],
        compiler_params=pltpu.CompilerParams(dimension_semantics=("arbitrary",),
                                             vmem_limit_bytes=VMEM_LIMIT_BYTES),
        name="front",
    )(xp2d, xs2d, g1, w1, w2, gm, win, qn2, kn2, *[w for w, _ in cast_srcs])


def _dense_weight_scratch(dense_w):
    return ([pltpu.VMEM(w.shape, w.dtype) for w in dense_w] + [pltpu.SemaphoreType.DMA((len(dense_w),))])


def _prompt_mix_call(sinks, x1, h, u, q, kv, small_w, dense_w, n_seq, seq_len):
    tm = ROW_TILE
    assert seq_len % tm == 0 and len(small_w) == 5 and len(dense_w) == len(DENSE_WEIGHTS)
    tiles = seq_len // tm
    row = lambda width: pl.BlockSpec((tm, width), lambda b, t: (b * tiles + t, 0))
    return pl.pallas_call(
        _prompt_mix_body,
        grid=(n_seq, tiles),
        in_specs=[pl.BlockSpec(memory_space=pltpu.SMEM),
                  row(D_MODEL), row(D_MODEL), row(D_POOL), row(D_Q), row(2 * D_KV)]
                 + [_const_spec(w.shape) for w in tuple(small_w) + tuple(dense_w)],
        out_specs=row(D_MODEL),
        out_shape=jax.ShapeDtypeStruct((n_seq * seq_len, D_MODEL), F32),
        scratch_shapes=[pltpu.VMEM((POOL_HIST_PAD + tm, D_POOL), F32),
                        pltpu.VMEM((4, WINDOW + tm, LANES), BF16),
                        pltpu.VMEM((4, WINDOW + tm, 2 * LANES), BF16),
                        pltpu.VMEM((tm, D_Q), BF16),
                        pltpu.VMEM((tm, D_FF), BF16)],
        compiler_params=pltpu.CompilerParams(dimension_semantics=("arbitrary", "arbitrary"),
                                             vmem_limit_bytes=VMEM_LIMIT_BYTES),
        name="prompt_mix",
    )(sinks, x1, h, u, q, kv, *small_w, *dense_w)


def _sample_tail_call(sinks, x1, h, u, q, kv, hist_u, hist_kv, small_w, dense_w, row_block, n_streams, t_new):
    tm = ROW_TILE
    assert n_streams * t_new == tm and t_new % BF16_SUBLANES == 0 and len(dense_w) == len(DENSE_WEIGHTS)
    row = lambda width: pl.BlockSpec((tm, width), lambda i: (row_block, 0))
    return pl.pallas_call(
        functools.partial(_sample_tail_body, n_streams=n_streams, t_new=t_new),
        grid=(1,),
        in_specs=[pl.BlockSpec(memory_space=pltpu.SMEM),
                  row(D_MODEL), row(D_MODEL), row(D_POOL), row(D_Q), row(2 * D_KV),
                  _const_spec(hist_u.shape), _const_spec(hist_kv.shape)]
                 + [_const_spec(w.shape) for w in small_w]
                 + [pl.BlockSpec(memory_space=pl.ANY) for _ in dense_w],
        out_specs=pl.BlockSpec((tm, D_MODEL), lambda i: (0, 0)),
        out_shape=jax.ShapeDtypeStruct((tm, D_MODEL), F32),
        scratch_shapes=[pltpu.VMEM((n_streams * (POOL_HIST_PAD + t_new), D_POOL), F32),
                        pltpu.VMEM((2, 4, WINDOW + t_new, LANES), BF16),
                        pltpu.VMEM((2, 4, WINDOW + t_new, 2 * LANES), BF16),
                        pltpu.VMEM((tm, D_Q), BF16),
                        pltpu.VMEM((tm, D_FF), BF16)] + _dense_weight_scratch(dense_w),
        compiler_params=pltpu.CompilerParams(dimension_semantics=("arbitrary",),
                                             vmem_limit_bytes=VMEM_LIMIT_BYTES),
        name="sample_tail",
    )(sinks, x1, h, u, q, kv, hist_u, hist_kv, *small_w, *dense_w)


def kernel(x_prompt, x_sample, state_pool, cache_k, cache_v, norm_ffn1, ffn1_w_in, ffn1_w_out, norm_mix, w_in, b_gate, pool_w, pool_b, pool_scale, q_norm, k_norm, sinks, w_pool_proj, w_attn_proj, w_out, norm_ffn2, ffn2_w_in, ffn2_w_out):
    depth = norm_ffn1.shape[0]
    assert depth == 1, "single-layer trunk"
    bp, sp, _ = x_prompt.shape
    bs, ts, _ = x_sample.shape
    n_p = bp * sp

    front_w = (norm_ffn1[0].reshape(1, D_MODEL), ffn1_w_in, ffn1_w_out, norm_mix[0].reshape(1, D_MODEL), w_in,
               jnp.tile(q_norm[0], 2).reshape(1, LANES), jnp.tile(k_norm[0], 2).reshape(1, LANES))
    cast_srcs = ((w_in, 2 * D_MODEL), (w_pool_proj, D_MODEL), (w_attn_proj, D_MODEL), (w_out, D_MODEL),
                 (ffn2_w_in, 2 * D_FF), (ffn2_w_out, D_MODEL))
    x1, h, u, q, kv, wg, wpp, wap, wo, ffn2_w1, ffn2_w2 = _front_call(
        x_prompt.reshape(n_p, D_MODEL), x_sample.reshape(bs * ts, D_MODEL), *front_w, cast_srcs=cast_srcs)

    pw = pool_w[0].astype(BF16)
    zero_blk = jnp.zeros((POOL_GROUP, POOL_GROUP), BF16)
    pw_pairs = jnp.stack([jnp.block([[pw[2 * i], zero_blk], [zero_blk, pw[2 * i + 1]]])
                          for i in range(N_POOL_GROUPS // 2)])
    small_w = (pw_pairs, pool_b[0].reshape(1, D_POOL), pool_scale[0].reshape(1, D_POOL),
               b_gate[0].reshape(1, 2 * D_MODEL), norm_ffn2[0].reshape(1, D_MODEL))
    dense_w = (wg, wpp, wap, wo, ffn2_w1, ffn2_w2)
    sink_vec = sinks[0]

    y_prompt = _prompt_mix_call(sink_vec, x1, h, u, q, kv, small_w, dense_w, bp, sp).reshape(bp, sp, D_MODEL)
    new_pool_p = jnp.stack([u[(b + 1) * sp - POOL_HIST:(b + 1) * sp] for b in range(bp)])
    kv_last = jnp.stack([kv[(b + 1) * sp - WINDOW:(b + 1) * sp] for b in range(bp)])
    new_k_p = kv_last[..., :D_KV].reshape(bp, WINDOW, N_KV_HEADS, HEAD_DIM)
    new_v_p = kv_last[..., D_KV:].reshape(bp, WINDOW, N_KV_HEADS, HEAD_DIM)

    hist_u = jnp.pad(state_pool[0], ((0, 0), (POOL_HIST_PAD - POOL_HIST, 0), (0, 0)))
    hist_kv = jnp.concatenate([cache_k[0].reshape(bs, WINDOW, D_KV), cache_v[0].reshape(bs, WINDOW, D_KV)], axis=-1)
    y_sample = _sample_tail_call(sink_vec, x1, h, u, q, kv, hist_u, hist_kv, small_w, dense_w,
                                 n_p // ROW_TILE, bs, ts).reshape(bs, ts, D_MODEL)
    us3 = u[n_p:].reshape(bs, ts, D_POOL)
    kvs3 = kv[n_p:].reshape(bs, ts, 2 * D_KV)
    new_pool_s = jnp.concatenate([state_pool[0], us3], axis=1)[:, -POOL_HIST:]
    kv_full = jnp.concatenate([hist_kv, kvs3], axis=1)[:, -WINDOW:]
    new_k_s = kv_full[..., :D_KV].reshape(bs, WINDOW, N_KV_HEADS, HEAD_DIM)
    new_v_s = kv_full[..., D_KV:].reshape(bs, WINDOW, N_KV_HEADS, HEAD_DIM)

    return (y_prompt, y_sample, new_pool_p[None], new_k_p[None], new_v_p[None],
            new_pool_s[None], new_k_s[None], new_v_s[None])
```

```python
import functools

import jax
import jax.numpy as jnp
from jax import lax
from jax.experimental import pallas as pl
from jax.experimental.pallas import tpu as pltpu

D_MODEL = 1024
CHUNK = 64
N_HEADS = 16
N_KV_HEADS = 2
HEAD_DIM = 64
WINDOW = 128
D_POOL = 512
N_POOL_GROUPS = 4
POOL_GROUP = 128
POOL_WINDOWS = (2, 4, 8, 16)
POOL_HIST = 15
POOL_HIST_PAD = 16
D_Q = N_HEADS * HEAD_DIM
D_KV = N_KV_HEADS * HEAD_DIM
D_UQKV = D_POOL + D_Q + 2 * D_KV
D_FF = 2816
EPS = 1e-6

LANES = 128
BF16_SUBLANES = 16
FF_CHUNK = 256
N_FF_CHUNKS = D_FF // FF_CHUNK
PAIRS_PER_KV = (N_HEADS // N_KV_HEADS) // 2
ROW_TILE = 512
PROJ_BLOCK_ORDER = (2, 3, 4, 5, 6, 0, 1)
WEIGHT_STAGE_SLOTS = (3, 2)
VMEM_LIMIT_BYTES = 58 * 1024 * 1024
NEG_BIG = float(jnp.finfo(jnp.float32).min)
LOG2E = 1.4426950408889634

BF16 = jnp.bfloat16
F32 = jnp.float32


def _dot(a, b):
    return jnp.dot(a, b, preferred_element_type=F32)


def _dot_nt(a, b):
    return lax.dot_general(a, b, (((1,), (1,)), ((), ())), preferred_element_type=F32)


def _rms_scale(x):
    return lax.rsqrt(jnp.mean(x * x, axis=-1, keepdims=True) + EPS)


def _ffn(x, g_ref, w1_ref, w2_ref, a_scr, side_work=(), loader=None):
    xg = (x * g_ref[...]).astype(BF16)
    r = _rms_scale(x)
    for c in range(N_FF_CHUNKS):
        if loader is not None:
            loader.before_ffn_chunk(c)
        gate = _dot(xg, w1_ref[:, c * FF_CHUNK:(c + 1) * FF_CHUNK]) * r
        up = _dot(xg, w1_ref[:, D_FF + c * FF_CHUNK:D_FF + (c + 1) * FF_CHUNK]) * r
        a_scr[:, c * FF_CHUNK:(c + 1) * FF_CHUNK] = (gate * jax.nn.sigmoid(gate) * up).astype(BF16)
        if c < len(side_work):
            side_work[c]()
    assert len(side_work) <= N_FF_CHUNKS
    if loader is not None:
        loader.before_ffn_out()
    return x + 0.5 * _dot(a_scr[...], w2_ref[...])


def _half_sumsq(x):
    y = x * x
    low = lax.broadcasted_iota(jnp.int32, y.shape, 1) < HEAD_DIM
    sum_low = jnp.sum(jnp.where(low, y, 0.0), axis=-1, keepdims=True)
    sum_high = jnp.sum(jnp.where(low, 0.0, y), axis=-1, keepdims=True)
    return jnp.where(low, sum_low, sum_high)


def _head_norm(x, g2):
    return x * lax.rsqrt(_half_sumsq(x) * (1.0 / HEAD_DIM) + EPS) * g2


def _cast_block(src_ref, dst_ref):
    dst_ref[...] = src_ref[:, src_ref.shape[1] - dst_ref.shape[1]:].astype(BF16)


class _WeightLoader:
    def __init__(self, w1_hbm, w2_hbm, win_hbm, w1_bf, w2_bf, win_bf, stage_cols, stage_rows, sem_cols, sem_rows):
        blk = FF_CHUNK
        self._units = {"cols": [], "rows": []}
        self._stage = {"cols": stage_cols, "rows": stage_rows}
        self._sem = {"cols": sem_cols, "rows": sem_rows}
        self._done = {"cols": 0, "rows": 0}

        def col_unit(hbm, dst, col):
            return (hbm.at[0, :, pl.ds(col, blk)], dst, (slice(None), slice(col, col + blk)))

        for c in range(N_FF_CHUNKS):
            self._units["cols"].append(col_unit(w1_hbm, w1_bf, c * blk))
            self._units["cols"].append(col_unit(w1_hbm, w1_bf, D_FF + c * blk))
            self._units["rows"].append((w2_hbm.at[0, pl.ds(c * blk, blk), :], w2_bf,
                                        (slice(c * blk, (c + 1) * blk), slice(None))))
        self._n_ffn_cols = len(self._units["cols"])
        for b in PROJ_BLOCK_ORDER:
            self._units["cols"].append(col_unit(win_hbm, win_bf, b * blk))

    def _copy(self, kind, k):
        n_slots = self._stage[kind].shape[0]
        src, _, _ = self._units[kind][k]
        return pltpu.make_async_copy(src, self._stage[kind].at[k % n_slots], self._sem[kind].at[k % n_slots])

    def start_all(self):
        for kind in ("cols", "rows"):
            for k in range(min(self._stage[kind].shape[0], len(self._units[kind]))):
                self._copy(kind, k).start(priority=k % 2)

    def _land(self, kind, upto):
        n_slots = self._stage[kind].shape[0]
        for k in range(self._done[kind], upto):
            self._copy(kind, k).wait()
            _, dst, idx = self._units[kind][k]
            dst[idx] = self._stage[kind][k % n_slots].astype(BF16)
            if k + n_slots < len(self._units[kind]):
                self._copy(kind, k + n_slots).start(priority=(k + n_slots) % 2)
        self._done[kind] = max(self._done[kind], upto)

    def before_ffn_chunk(self, c):
        self._land("cols", 2 * (c + 1))
        self._land("rows", c)

    def before_ffn_out(self):
        self._land("rows", N_FF_CHUNKS)

    def before_proj_blocks(self, n_used):
        self._land("cols", self._n_ffn_cols + n_used)

    def finished(self):
        return all(self._done[kind] == len(self._units[kind]) for kind in self._units)


def _front_compute(x, g1_ref, w1_bf, w2_bf, gm_ref, win_bf, qn_ref, kn_ref,
                   x1_ref, h_ref, u_ref, q_ref, kv_ref, a_scr, casts, loader):
    x1 = _ffn(x, g1_ref, w1_bf, w2_bf, a_scr, casts, loader)
    x1_ref[...] = x1
    xg = x1 * gm_ref[...]
    r = _rms_scale(x1)
    h_ref[...] = (xg * r).astype(BF16)
    h = xg.astype(BF16)
    qg = qn_ref[...] * (HEAD_DIM ** -0.5 * LOG2E)
    blk = 2 * LANES
    for b in range(D_Q // blk):
        if loader is not None:
            loader.before_proj_blocks(b + 1)
        zq = _dot(h, win_bf[:, D_POOL + b * blk:D_POOL + (b + 1) * blk]) * r
        for p in range(2):
            q_ref[:, b * blk + p * LANES:b * blk + (p + 1) * LANES] = _head_norm(
                zq[:, p * LANES:(p + 1) * LANES], qg).astype(BF16)
    if loader is not None:
        loader.before_proj_blocks(D_Q // blk + 1)
    zkv = _dot(h, win_bf[:, D_POOL + D_Q:D_POOL + D_Q + 2 * D_KV]) * r
    kv_ref[:, :D_KV] = _head_norm(zkv[:, :D_KV], kn_ref[...])
    kv_ref[:, D_KV:] = zkv[:, D_KV:]
    if loader is not None:
        loader.before_proj_blocks(len(PROJ_BLOCK_ORDER))
    u_ref[...] = _dot(h, win_bf[:, :D_POOL]) * r


def _front_body(*refs, n_cast):
    xp_ref, xs_ref, g1_ref, w1_hbm, w2_hbm, gm_ref, win_hbm, qn_ref, kn_ref = refs[:9]
    cast_src = refs[9:9 + n_cast]
    outs = refs[9 + n_cast:14 + n_cast]
    cast_dst = refs[14 + n_cast:14 + 2 * n_cast]
    a_scr, w1_bf, w2_bf, win_bf, stage_cols, stage_rows, sem_cols, sem_rows = refs[14 + 2 * n_cast:]
    casts = [functools.partial(_cast_block, src, dst) for src, dst in zip(cast_src, cast_dst)]
    i = pl.program_id(0)
    weights = (g1_ref, w1_bf, w2_bf, gm_ref, win_bf, qn_ref, kn_ref)

    @pl.when(i == 0)
    def _():
        loader = _WeightLoader(w1_hbm, w2_hbm, win_hbm, w1_bf, w2_bf, win_bf,
                               stage_cols, stage_rows, sem_cols, sem_rows)
        loader.start_all()
        _front_compute(xs_ref[...], *weights, *outs, a_scr, casts, loader)
        assert loader.finished()

    @pl.when(i > 0)
    def _():
        _front_compute(xp_ref[...], *weights, *outs, a_scr, casts, None)


def _pool_pair(ext_ref, take_new, first_row, pw_ref, pb_ref, ps_ref, pair):
    pooled = []
    for gi in (2 * pair, 2 * pair + 1):
        w = POOL_WINDOWS[gi]
        ext = ext_ref[:, gi * POOL_GROUP:(gi + 1) * POOL_GROUP]
        u_new = take_new(ext)
        s = ext
        k = 1
        while k < w:
            s = s + pltpu.roll(s, k, axis=0)
            k *= 2
        s = take_new(s)
        if first_row is None:
            mean = s * (1.0 / w)
        else:
            pos = first_row + lax.broadcasted_iota(jnp.int32, (s.shape[0], 1), 0)
            mean = s * (1.0 / jnp.minimum(pos + 1, w).astype(F32))
        pooled.append((mean - u_new).astype(BF16))
    cols = slice(2 * pair * POOL_GROUP, 2 * (pair + 1) * POOL_GROUP)
    return (_dot(jnp.concatenate(pooled, axis=-1), pw_ref[pair]) + pb_ref[:, cols]) * ps_ref[:, cols]


def _pool_mixer(ext_ref, take_new, first_row, pw_ref, pb_ref, ps_ref):
    return jnp.concatenate([_pool_pair(ext_ref, take_new, first_row, pw_ref, pb_ref, ps_ref, i).astype(BF16)
                            for i in range(N_POOL_GROUPS // 2)], axis=-1)


def _fill_kv_variants(kx_ref, vx_ref, row0, kv):
    rows = kv.shape[0]
    low = lax.broadcasted_iota(jnp.int32, (rows, LANES), 1) < HEAD_DIM
    ones = (jnp.where(low, 1.0, 0.0).astype(BF16), jnp.where(low, 0.0, 1.0).astype(BF16))
    for t, base in ((0, 0), (1, D_KV)):
        both = kv[:, base:base + D_KV]
        swapped = pltpu.roll(both, HEAD_DIM, axis=1)
        variants = (jnp.where(low, both, 0.0), jnp.where(low, 0.0, swapped),
                    jnp.where(low, swapped, 0.0), jnp.where(low, 0.0, both))
        for i, val in enumerate(variants):
            if t == 0:
                kx_ref[i, row0:row0 + rows, :] = val.astype(BF16)
            else:
                vx_ref[i, row0:row0 + rows, :LANES] = val.astype(BF16)
                vx_ref[i, row0:row0 + rows, LANES:] = ones[i % 2]


def _sink_columns(sinks_ref, q_rows):
    m_rows = PAIRS_PER_KV * q_rows
    row_blk = lax.broadcasted_iota(jnp.int32, (m_rows, 1), 0) // q_rows
    cols = []
    for kh in range(N_KV_HEADS):
        pair = []
        for ab in range(2):
            sink = jnp.zeros((m_rows, 1), F32)
            for pp in range(PAIRS_PER_KV):
                sink = jnp.where(row_blk == pp, sinks_ref[kh * 2 * PAIRS_PER_KV + 2 * pp + ab] * LOG2E, sink)
            pair.append(sink)
        cols.append(pair)
    return cols


def _attend_block(q_ref, q_row0, q_rows, kx_ref, vx_ref, key_row0, n_keys, key_valid, sink_cols, yb_ref):
    m_rows = PAIRS_PER_KV * q_rows
    low = lax.broadcasted_iota(jnp.int32, (m_rows, LANES), 1) < HEAD_DIM
    for kh in range(N_KV_HEADS):
        qs = jnp.concatenate(
            [q_ref[q_row0:q_row0 + q_rows, (kh * PAIRS_PER_KV + pp) * LANES:(kh * PAIRS_PER_KV + pp + 1) * LANES]
             for pp in range(PAIRS_PER_KV)], axis=0)
        o = None
        row_max = []
        for ab in range(2):
            s = _dot_nt(qs, kx_ref[2 * kh + ab, key_row0:key_row0 + n_keys, :])
            if key_valid is not None:
                s = jnp.where(key_valid, s, NEG_BIG)
            m = jnp.maximum(jnp.max(s, axis=-1, keepdims=True), sink_cols[kh][ab])
            p = jnp.exp2(s - m).astype(BF16)
            pv = _dot(p, vx_ref[2 * kh + ab, key_row0:key_row0 + n_keys, :])
            o = pv if o is None else o + pv
            row_max.append(m)
        sink_p = jnp.exp2(jnp.where(low, sink_cols[kh][0], sink_cols[kh][1])
                          - jnp.where(low, row_max[0], row_max[1]))
        out = o[:, :LANES] / (o[:, LANES:] + sink_p)
        for pp in range(PAIRS_PER_KV):
            col = (kh * PAIRS_PER_KV + pp) * LANES
            yb_ref[q_row0:q_row0 + q_rows, col:col + LANES] = out[pp * q_rows:(pp + 1) * q_rows].astype(BF16)


DENSE_WEIGHTS = ("wg", "wpp", "wap", "wo", "w1", "w2")


class _LazyWeights:
    def __init__(self, hbm_refs, vmem_refs, sem):
        assert len(hbm_refs) == len(vmem_refs) == len(DENSE_WEIGHTS)
        self._copies = [pltpu.make_async_copy(src, dst, sem.at[k])
                        for k, (src, dst) in enumerate(zip(hbm_refs, vmem_refs))]
        self._waited = set()

    def start_all(self):
        for copy in self._copies:
            copy.start()

    def wait(self, name):
        k = DENSE_WEIGHTS.index(name)
        if k not in self._waited:
            self._copies[k].wait()
            self._waited.add(k)

    def before_ffn_chunk(self, c):
        if c == 0:
            self.wait("w1")

    def before_ffn_out(self):
        self.wait("w2")

    def finished(self):
        return len(self._waited) == len(self._copies)


def _dense_tail(x1, h, ya, yb, wg_ref, bg_ref, wpp_ref, wap_ref, wo_ref, g2_ref, w1_ref, w2_ref, a_scr, lazy=None):
    wait = lazy.wait if lazy is not None else (lambda name: None)
    wait("wg")
    gates = jax.nn.sigmoid(_dot(h, wg_ref[...]) + bg_ref[...])
    wait("wpp")
    a = _dot(ya, wpp_ref[...])
    wait("wap")
    b = _dot(yb, wap_ref[...])
    mixed = gates[:, :D_MODEL] * a + gates[:, D_MODEL:] * b
    wait("wo")
    x2 = x1 + _dot(mixed.astype(BF16), wo_ref[...])
    return _ffn(x2, g2_ref, w1_ref, w2_ref, a_scr, loader=lazy)


def _prompt_mix_tile(sinks_ref, x1_ref, h_ref, u_ref, q_ref, kv_ref, pw_ref, pb_ref, ps_ref, bg_ref, g2_ref,
                     dense_w, o_ref, uext, kx, vx, yb_scr, a_scr, lazy):
    rows = x1_ref.shape[0]
    t = pl.program_id(1)
    wg_ref, wpp_ref, wap_ref, wo_ref, w1_ref, w2_ref = dense_w

    @pl.when(t == 0)
    def _():
        uext[:POOL_HIST_PAD, :] = jnp.zeros((POOL_HIST_PAD, D_POOL), F32)
        kx[:, :WINDOW, :] = jnp.zeros((4, WINDOW, LANES), BF16)
        vx[:, :WINDOW, :] = jnp.zeros((4, WINDOW, 2 * LANES), BF16)

    uext[POOL_HIST_PAD:, :] = u_ref[...]
    _fill_kv_variants(kx, vx, WINDOW, kv_ref[...])

    ya = _pool_mixer(uext, lambda a: a[POOL_HIST_PAD:], t * rows, pw_ref, pb_ref, ps_ref)

    n_chunks = rows // CHUNK
    n_keys = WINDOW + CHUNK
    key_chunk = lax.broadcasted_iota(jnp.int32, (1, n_keys), 1) // CHUNK
    sink_cols = _sink_columns(sinks_ref, CHUNK)
    for j in range(n_chunks):
        first_chunk = t * n_chunks + j - WINDOW // CHUNK
        key_valid = key_chunk + first_chunk >= 0 if j < WINDOW // CHUNK else None
        _attend_block(q_ref, j * CHUNK, CHUNK, kx, vx, j * CHUNK, n_keys, key_valid, sink_cols, yb_scr)

    o_ref[...] = _dense_tail(x1_ref[...], h_ref[...], ya, yb_scr[...], wg_ref, bg_ref, wpp_ref, wap_ref, wo_ref,
                             g2_ref, w1_ref, w2_ref, a_scr, lazy)

    uext[:POOL_HIST_PAD, :] = uext[rows:rows + POOL_HIST_PAD, :]
    kx[:, :WINDOW, :] = kx[:, rows:rows + WINDOW, :]
    vx[:, :WINDOW, :] = vx[:, rows:rows + WINDOW, :]


def _prompt_mix_body(*refs):
    n_small = 11
    n_w = len(DENSE_WEIGHTS)
    _prompt_mix_tile(*refs[:n_small], refs[n_small:n_small + n_w], *refs[n_small + n_w:], None)


def _sample_tail_body(*refs, n_streams, t_new):
    n_w = len(DENSE_WEIGHTS)
    (sinks_ref, x1_ref, h_ref, u_ref, q_ref, kv_ref, hu_ref, hkv_ref,
     pw_ref, pb_ref, ps_ref, bg_ref, g2_ref) = refs[:13]
    dense_hbm = refs[13:13 + n_w]
    o_ref, uext, kx, vx, yb_scr, a_scr = refs[13 + n_w:19 + n_w]
    wg_ref, wpp_ref, wap_ref, wo_ref, w1_ref, w2_ref = dense_vmem = refs[19 + n_w:19 + 2 * n_w]
    lazy = _LazyWeights(dense_hbm, dense_vmem, refs[19 + 2 * n_w])
    lazy.start_all()

    seg = POOL_HIST_PAD + t_new
    for e in range(n_streams):
        uext[e * seg:e * seg + POOL_HIST_PAD, :] = hu_ref[e]
        uext[e * seg + POOL_HIST_PAD:(e + 1) * seg, :] = u_ref[e * t_new:(e + 1) * t_new, :]

    def take_new(a):
        return jnp.concatenate([a[e * seg + POOL_HIST_PAD:(e + 1) * seg] for e in range(n_streams)], axis=0)

    ya = _pool_mixer(uext, take_new, None, pw_ref, pb_ref, ps_ref)

    sink_cols = _sink_columns(sinks_ref, t_new)
    for e in range(n_streams):
        kx_e, vx_e = kx.at[e % 2], vx.at[e % 2]
        _fill_kv_variants(kx_e, vx_e, 0, hkv_ref[e])
        _fill_kv_variants(kx_e, vx_e, WINDOW, kv_ref[e * t_new:(e + 1) * t_new, :])
        _attend_block(q_ref, e * t_new, t_new, kx_e, vx_e, 0, WINDOW + t_new, None, sink_cols, yb_scr)

    o_ref[...] = _dense_tail(x1_ref[...], h_ref[...], ya, yb_scr[...], wg_ref, bg_ref, wpp_ref, wap_ref, wo_ref,
                             g2_ref, w1_ref, w2_ref, a_scr, lazy)
    assert lazy.finished()


def _const_spec(shape):
    return pl.BlockSpec(shape, lambda *_: (0,) * len(shape), pipeline_mode=pl.Buffered(1))


def _front_call(xp2d, xs2d, g1, w1, w2, gm, win, qn2, kn2, cast_srcs):
    n_p = xp2d.shape[0]
    tm = ROW_TILE
    assert n_p % tm == 0 and xs2d.shape[0] == tm
    n_prompt_steps = n_p // tm
    steps = n_prompt_steps + 1
    n = n_p + tm
    row = lambda width: pl.BlockSpec((tm, width), lambda i: (jnp.where(i == 0, n_prompt_steps, i - 1), 0))
    cast_in, cast_out, cast_shapes = [], [], []
    for w, cols in cast_srcs:
        _, w_rows, w_cols = w.shape
        n_blk = max(d for d in range(1, steps + 1) if w_rows % (BF16_SUBLANES * d) == 0)
        rows = w_rows // n_blk
        assert cols % LANES == 0
        cast_in.append(pl.BlockSpec((None, rows, w_cols), lambda i, n_blk=n_blk: (0, jnp.minimum(i, n_blk - 1), 0)))
        cast_out.append(pl.BlockSpec((rows, cols), lambda i, n_blk=n_blk: (jnp.minimum(i, n_blk - 1), 0)))
        cast_shapes.append(jax.ShapeDtypeStruct((w_rows, cols), BF16))
    hbm = pl.BlockSpec(memory_space=pl.ANY)
    col_slots, row_slots = WEIGHT_STAGE_SLOTS
    return pl.pallas_call(
        functools.partial(_front_body, n_cast=len(cast_srcs)),
        grid=(steps,),
        in_specs=[pl.BlockSpec((tm, D_MODEL), lambda i: (jnp.maximum(i - 1, 0), 0)),
                  _const_spec(xs2d.shape),
                  _const_spec(g1.shape), hbm, hbm, _const_spec(gm.shape), hbm,
                  _const_spec(qn2.shape), _const_spec(kn2.shape)]
                 + cast_in,
        out_specs=[row(D_MODEL), row(D_MODEL), row(D_POOL), row(D_Q), row(2 * D_KV)] + cast_out,
        out_shape=[jax.ShapeDtypeStruct((n, D_MODEL), F32), jax.ShapeDtypeStruct((n, D_MODEL), BF16),
                   jax.ShapeDtypeStruct((n, D_POOL), F32), jax.ShapeDtypeStruct((n, D_Q), BF16),
                   jax.ShapeDtypeStruct((n, 2 * D_KV), F32)] + cast_shapes,
        scratch_shapes=[pltpu.VMEM((tm, D_FF), BF16),
                        pltpu.VMEM((D_MODEL, 2 * D_FF), BF16), pltpu.VMEM((D_FF, D_MODEL), BF16),
                        pltpu.VMEM((D_MODEL, D_UQKV), BF16),
                        pltpu.VMEM((col_slots, D_MODEL, FF_CHUNK), F32), pltpu.VMEM((row_slots, FF_CHUNK, D_MODEL), F32),
                        pltpu.SemaphoreType.DMA((col_slots,)), pltpu.SemaphoreType.DMA((row_slots,))],
        compiler_params=pltpu.CompilerParams(dimension_semantics=("arbitrary",),
                                             vmem_limit_bytes=VMEM_LIMIT_BYTES),
        name="front",
    )(xp2d, xs2d, g1, w1, w2, gm, win, qn2, kn2, *[w for w, _ in cast_srcs])


def _dense_weight_scratch(dense_w):
    return ([pltpu.VMEM(w.shape, w.dtype) for w in dense_w] + [pltpu.SemaphoreType.DMA((len(dense_w),))])


def _prompt_mix_call(sinks, x1, h, u, q, kv, small_w, dense_w, n_seq, seq_len):
    tm = ROW_TILE
    assert seq_len % tm == 0 and len(small_w) == 5 and len(dense_w) == len(DENSE_WEIGHTS)
    tiles = seq_len // tm
    row = lambda width: pl.BlockSpec((tm, width), lambda b, t: (b * tiles + t, 0))
    return pl.pallas_call(
        _prompt_mix_body,
        grid=(n_seq, tiles),
        in_specs=[pl.BlockSpec(memory_space=pltpu.SMEM),
                  row(D_MODEL), row(D_MODEL), row(D_POOL), row(D_Q), row(2 * D_KV)]
                 + [_const_spec(w.shape) for w in tuple(small_w) + tuple(dense_w)],
        out_specs=row(D_MODEL),
        out_shape=jax.ShapeDtypeStruct((n_seq * seq_len, D_MODEL), F32),
        scratch_shapes=[pltpu.VMEM((POOL_HIST_PAD + tm, D_POOL), F32),
                        pltpu.VMEM((4, WINDOW + tm, LANES), BF16),
                        pltpu.VMEM((4, WINDOW + tm, 2 * LANES), BF16),
                        pltpu.VMEM((tm, D_Q), BF16),
                        pltpu.VMEM((tm, D_FF), BF16)],
        compiler_params=pltpu.CompilerParams(dimension_semantics=("arbitrary", "arbitrary"),
                                             vmem_limit_bytes=VMEM_LIMIT_BYTES),
        name="prompt_mix",
    )(sinks, x1, h, u, q, kv, *small_w, *dense_w)


def _sample_tail_call(sinks, x1, h, u, q, kv, hist_u, hist_kv, small_w, dense_w, row_block, n_streams, t_new):
    tm = ROW_TILE
    assert n_streams * t_new == tm and t_new % BF16_SUBLANES == 0 and len(dense_w) == len(DENSE_WEIGHTS)
    row = lambda width: pl.BlockSpec((tm, width), lambda i: (row_block, 0))
    return pl.pallas_call(
        functools.partial(_sample_tail_body, n_streams=n_streams, t_new=t_new),
        grid=(1,),
        in_specs=[pl.BlockSpec(memory_space=pltpu.SMEM),
                  row(D_MODEL), row(D_MODEL), row(D_POOL), row(D_Q), row(2 * D_KV),
                  _const_spec(hist_u.shape), _const_spec(hist_kv.shape)]
                 + [_const_spec(w.shape) for w in small_w]
                 + [pl.BlockSpec(memory_space=pl.ANY) for _ in dense_w],
        out_specs=pl.BlockSpec((tm, D_MODEL), lambda i: (0, 0)),
        out_shape=jax.ShapeDtypeStruct((tm, D_MODEL), F32),
        scratch_shapes=[pltpu.VMEM((n_streams * (POOL_HIST_PAD + t_new), D_POOL), F32),
                        pltpu.VMEM((2, 4, WINDOW + t_new, LANES), BF16),
                        pltpu.VMEM((2, 4, WINDOW + t_new, 2 * LANES), BF16),
                        pltpu.VMEM((tm, D_Q), BF16),
                        pltpu.VMEM((tm, D_FF), BF16)] + _dense_weight_scratch(dense_w),
        compiler_params=pltpu.CompilerParams(dimension_semantics=("arbitrary",),
                                             vmem_limit_bytes=VMEM_LIMIT_BYTES),
        name="sample_tail",
    )(sinks, x1, h, u, q, kv, hist_u, hist_kv, *small_w, *dense_w)


def kernel(x_prompt, x_sample, state_pool, cache_k, cache_v, norm_ffn1, ffn1_w_in, ffn1_w_out, norm_mix, w_in, b_gate, pool_w, pool_b, pool_scale, q_norm, k_norm, sinks, w_pool_proj, w_attn_proj, w_out, norm_ffn2, ffn2_w_in, ffn2_w_out):
    depth = norm_ffn1.shape[0]
    assert depth == 1, "single-layer trunk"
    bp, sp, _ = x_prompt.shape
    bs, ts, _ = x_sample.shape
    n_p = bp * sp

    front_w = (norm_ffn1[0].reshape(1, D_MODEL), ffn1_w_in, ffn1_w_out, norm_mix[0].reshape(1, D_MODEL), w_in,
               jnp.tile(q_norm[0], 2).reshape(1, LANES), jnp.tile(k_norm[0], 2).reshape(1, LANES))
    cast_srcs = ((w_in, 2 * D_MODEL), (w_pool_proj, D_MODEL), (w_attn_proj, D_MODEL), (w_out, D_MODEL),
                 (ffn2_w_in, 2 * D_FF), (ffn2_w_out, D_MODEL))
    x1, h, u, q, kv, wg, wpp, wap, wo, ffn2_w1, ffn2_w2 = _front_call(
        x_prompt.reshape(n_p, D_MODEL), x_sample.reshape(bs * ts, D_MODEL), *front_w, cast_srcs=cast_srcs)

    pw = pool_w[0].astype(BF16)
    zero_blk = jnp.zeros((POOL_GROUP, POOL_GROUP), BF16)
    pw_pairs = jnp.stack([jnp.block([[pw[2 * i], zero_blk], [zero_blk, pw[2 * i + 1]]])
                          for i in range(N_POOL_GROUPS // 2)])
    small_w = (pw_pairs, pool_b[0].reshape(1, D_POOL), pool_scale[0].reshape(1, D_POOL),
               b_gate[0].reshape(1, 2 * D_MODEL), norm_ffn2[0].reshape(1, D_MODEL))
    dense_w = (wg, wpp, wap, wo, ffn2_w1, ffn2_w2)
    sink_vec = sinks[0]

    y_prompt = _prompt_mix_call(sink_vec, x1, h, u, q, kv, small_w, dense_w, bp, sp).reshape(bp, sp, D_MODEL)
    new_pool_p = jnp.stack([u[(b + 1) * sp - POOL_HIST:(b + 1) * sp] for b in range(bp)])
    kv_last = jnp.stack([kv[(b + 1) * sp - WINDOW:(b + 1) * sp] for b in range(bp)])
    new_k_p = kv_last[..., :D_KV].reshape(bp, WINDOW, N_KV_HEADS, HEAD_DIM)
    new_v_p = kv_last[..., D_KV:].reshape(bp, WINDOW, N_KV_HEADS, HEAD_DIM)

    hist_u = jnp.pad(state_pool[0], ((0, 0), (POOL_HIST_PAD - POOL_HIST, 0), (0, 0)))
    hist_kv = jnp.concatenate([cache_k[0].reshape(bs, WINDOW, D_KV), cache_v[0].reshape(bs, WINDOW, D_KV)], axis=-1)
    y_sample = _sample_tail_call(sink_vec, x1, h, u, q, kv, hist_u, hist_kv, small_w, dense_w,
                                 n_p // ROW_TILE, bs, ts).reshape(bs, ts, D_MODEL)
    us3 = u[n_p:].reshape(bs, ts, D_POOL)
    kvs3 = kv[n_p:].reshape(bs, ts, 2 * D_KV)
    new_pool_s = jnp.concatenate([state_pool[0], us3], axis=1)[:, -POOL_HIST:]
    kv_full = jnp.concatenate([hist_kv, kvs3], axis=1)[:, -WINDOW:]
    new_k_s = kv_full[..., :D_KV].reshape(bs, WINDOW, N_KV_HEADS, HEAD_DIM)
    new_v_s = kv_full[..., D_KV:].reshape(bs, WINDOW, N_KV_HEADS, HEAD_DIM)

    return (y_prompt, y_sample, new_pool_p[None], new_k_p[None], new_v_p[None],
            new_pool_s[None], new_k_s[None], new_v_s[None])
```
